```python
import math
import jax
import jax.numpy as jnp
from jax import lax
import numpy as np

D_MODEL = 1024
BATCH = 8
SEQ = 8192
DEPTH = 2

CTX_LEN = 256
GRID_W = 64
N_MOD = 6
CONV_DIM = 512
CONV_K = 31
MLA_HEADS = 8
Q_LORA = 384
KV_LORA = 256
QK_NOPE = 64
QK_ROPE = 32
V_HEAD = 64
QK_HEAD = QK_NOPE + QK_ROPE
AXIS_PAIRS = QK_ROPE // 4
ROPE_THETA = 10000.0
SOFTMAX_SCALE = 1.0 / math.sqrt(QK_HEAD)
Q_BLOCK = 128
GMLP_DIM = 512
GMLP_GROUPS = 8
GMLP_GROUP_DIM = GMLP_DIM // GMLP_GROUPS
CHUNK = 128
N_BRANCH = 3
COL_Q = 2 * CONV_DIM
COL_KV = COL_Q + Q_LORA
COL_KR = COL_KV + KV_LORA
COL_GM = COL_KR + QK_ROPE
COL_GATE = COL_GM + 2 * GMLP_DIM
IN_DIM = COL_GATE + N_BRANCH * D_MODEL
D_FF = 2816
N_EXPERTS = 8
TOP_K = 2
D_FF_EXPERT = 3584
MOE_BLOCK = 128
N_DENSE_LAYERS = (DEPTH + 1) // 2
N_MOE_LAYERS = DEPTH // 2
RMS_EPS = 1e-6
LN_EPS = 1e-5

kernel_name = "hybrid_conv_mla_gmlp_moe_dit"


def rms_norm(x, g):
    xf = x.astype(jnp.float32)
    y = xf * lax.rsqrt(jnp.mean(xf * xf, axis=-1, keepdims=True) + RMS_EPS)
    return (y * g.astype(jnp.float32)).astype(x.dtype)


def layer_norm(x, g, b):
    xf = x.astype(jnp.float32)
    xc = xf - jnp.mean(xf, axis=-1, keepdims=True)
    var = jnp.mean(xc * xc, axis=-1, keepdims=True)
    y = xc * lax.rsqrt(var + LN_EPS) * g.astype(jnp.float32) + b.astype(jnp.float32)
    return y.astype(x.dtype)


def modulate(h, shift, scale):
    return h * (1 + scale) + shift


def axial_angles(rows):
    row = jnp.repeat(jnp.arange(rows, dtype=jnp.float32), GRID_W)
    col = jnp.tile(jnp.arange(GRID_W, dtype=jnp.float32), rows)
    inv_freq = ROPE_THETA ** (-jnp.arange(AXIS_PAIRS, dtype=jnp.float32) / AXIS_PAIRS)
    return row[:, None] * inv_freq, col[:, None] * inv_freq


def rotate_pairs(x, ang):
    x1, x2 = jnp.split(x, 2, axis=-1)
    cos = jnp.cos(ang).astype(x.dtype)
    sin = jnp.sin(ang).astype(x.dtype)
    return jnp.concatenate([x1 * cos - x2 * sin, x1 * sin + x2 * cos], axis=-1)


def axial_rope(x, ang_row, ang_col):
    xr, xc = jnp.split(x, 2, axis=-1)
    return jnp.concatenate([rotate_pairs(xr, ang_row[None, :, None, :]),
                            rotate_pairs(xc, ang_col[None, :, None, :])], axis=-1)


def mla_query(c_q, q_norm_g, w_uq, angles):
    b, n, _ = c_q.shape
    q = (rms_norm(c_q, q_norm_g) @ w_uq).reshape(b, n, MLA_HEADS, QK_HEAD)
    q_nope, q_rope = q[..., :QK_NOPE], q[..., QK_NOPE:]
    if angles is not None:
        q_rope = axial_rope(q_rope, angles[0], angles[1])
    return jnp.concatenate([q_nope, q_rope], axis=-1)


def mla_key_value(c_kv, k_rope, kv_norm_g, w_ukv, angles):
    b, n, _ = c_kv.shape
    kv = (rms_norm(c_kv, kv_norm_g) @ w_ukv).reshape(b, n, MLA_HEADS, QK_NOPE + V_HEAD)
    k_nope, v = kv[..., :QK_NOPE], kv[..., QK_NOPE:]
    k_rope = k_rope[:, :, None, :]
    if angles is not None:
        k_rope = axial_rope(k_rope, angles[0], angles[1])
    k = jnp.concatenate([k_nope, jnp.broadcast_to(k_rope, (b, n, MLA_HEADS, QK_ROPE))], axis=-1)
    return k, v


def attend(q, k, v):
    s = jnp.einsum('bqhd,bkhd->bhqk', q, k).astype(jnp.float32) * SOFTMAX_SCALE
    p = jax.nn.softmax(s, axis=-1).astype(v.dtype)
    return jnp.einsum('bhqk,bkhd->bqhd', p, v)


def latent_attention(q, k_all, v_all):
    b, n, h, dk = q.shape
    q_blocks = q.reshape(b, n // Q_BLOCK, Q_BLOCK, h, dk).transpose(1, 0, 2, 3, 4)
    o = lax.map(lambda qb: attend(qb, k_all, v_all), q_blocks)
    return o.transpose(1, 0, 2, 3, 4).reshape(b, n, h * V_HEAD)


def conformer_conv(z, dw, db, ln_g, ln_b, w_proj):
    a, gate = jnp.split(z, 2, axis=-1)
    u = a * jax.nn.sigmoid(gate)
    u = lax.conv_general_dilated(u, dw[:, None, :].astype(u.dtype), window_strides=(1,),
                                 padding=[(CONV_K // 2, CONV_K // 2)],
                                 dimension_numbers=('NWC', 'WIO', 'NWC'),
                                 feature_group_count=CONV_DIM) + db
    u = jax.nn.silu(layer_norm(u, ln_g, ln_b))
    return u @ w_proj


def chunk_gmlp(z, ln_g, ln_b, w_s, b_s, w_proj):
    b, n, _ = z.shape
    u, v = jnp.split(jax.nn.gelu(z), 2, axis=-1)
    v = layer_norm(v, ln_g, ln_b).reshape(b, n // CHUNK, CHUNK, GMLP_GROUPS, GMLP_GROUP_DIM)
    s = jnp.einsum('gpq,bcqgd->bcpgd', w_s, v) + b_s.T[:, :, None]
    return (u * s.reshape(b, n, GMLP_DIM)) @ w_proj


def split_columns(z):
    return jnp.split(z, [COL_Q, COL_KV, COL_KR, COL_GM, COL_GATE], axis=-1)


def mixer_output(zs, attn_heads, b_gate, conv_dw, conv_db, conv_ln_g, conv_ln_b, w_conv_out,
                 w_attn_out, gmlp_ln_g, gmlp_ln_b, w_spatial, b_spatial, w_gmlp_out, w_out):
    z_conv, z_gmlp, z_gate = zs[0], zs[4], zs[5]
    br_conv = conformer_conv(z_conv, conv_dw, conv_db, conv_ln_g, conv_ln_b, w_conv_out)
    br_attn = attn_heads @ w_attn_out
    br_gmlp = chunk_gmlp(z_gmlp, gmlp_ln_g, gmlp_ln_b, w_spatial, b_spatial, w_gmlp_out)
    gates = jax.nn.sigmoid((z_gate + b_gate).astype(jnp.float32)).astype(z_gate.dtype)
    g_conv, g_attn, g_gmlp = jnp.split(gates, N_BRANCH, axis=-1)
    return (g_conv * br_conv + g_attn * br_attn + g_gmlp * br_gmlp) @ w_out


def token_mixer(a_lat, a_ctx, angles, need_ctx, w_in, b_gate, conv_dw, conv_db, conv_ln_g,
                conv_ln_b, w_conv_out, q_norm_g, w_uq, kv_norm_g, w_ukv, w_attn_out, gmlp_ln_g,
                gmlp_ln_b, w_spatial, b_spatial, w_gmlp_out, w_out):
    zl = split_columns(a_lat @ w_in)
    zc = split_columns(a_ctx @ w_in)
    k_ctx, v_ctx = mla_key_value(zc[2], zc[3], kv_norm_g, w_ukv, None)
    k_lat, v_lat = mla_key_value(zl[2], zl[3], kv_norm_g, w_ukv, angles)
    q_lat = mla_query(zl[1], q_norm_g, w_uq, angles)
    o_lat = latent_attention(q_lat, jnp.concatenate([k_lat, k_ctx], axis=1),
                             jnp.concatenate([v_lat, v_ctx], axis=1))
    y_lat = mixer_output(zl, o_lat, b_gate, conv_dw, conv_db, conv_ln_g, conv_ln_b, w_conv_out,
                         w_attn_out, gmlp_ln_g, gmlp_ln_b, w_spatial, b_spatial, w_gmlp_out, w_out)
    y_ctx = None
    if need_ctx:
        b, n_ctx = a_ctx.shape[0], a_ctx.shape[1]
        q_ctx = mla_query(zc[1], q_norm_g, w_uq, None)
        o_ctx = attend(q_ctx, k_ctx, v_ctx).reshape(b, n_ctx, MLA_HEADS * V_HEAD)
        y_ctx = mixer_output(zc, o_ctx, b_gate, conv_dw, conv_db, conv_ln_g, conv_ln_b,
                             w_conv_out, w_attn_out, gmlp_ln_g, gmlp_ln_b, w_spatial, b_spatial,
                             w_gmlp_out, w_out)
    return y_lat, y_ctx


def swiglu(h, w13, w2):
    a, g = jnp.split(h @ w13, 2, axis=-1)
    return (jax.nn.silu(a) * g) @ w2


def moe_swiglu(h, w_router, w13, w2):
    n_tok, d = h.shape
    n_assign = n_tok * TOP_K
    logits = (h @ w_router).astype(jnp.float32)
    top_logit, top_expert = lax.top_k(logits, TOP_K)
    top_w = jax.nn.softmax(top_logit, axis=-1).astype(h.dtype)
    expert = top_expert.reshape(-1)
    token = jnp.repeat(jnp.arange(n_tok), TOP_K)
    order = jnp.argsort(expert)
    expert_s, token_s, w_s = expert[order], token[order], top_w.reshape(-1)[order]
    counts = jnp.bincount(expert, length=N_EXPERTS)
    starts = jnp.cumsum(counts) - counts
    padded = (counts + MOE_BLOCK - 1) // MOE_BLOCK * MOE_BLOCK
    padded_end = jnp.cumsum(padded)
    dest = (padded_end - padded)[expert_s] + jnp.arange(n_assign) - starts[expert_s]
    n_rows = -(-n_assign // MOE_BLOCK) * MOE_BLOCK + N_EXPERTS * MOE_BLOCK
    n_blocks = n_rows // MOE_BLOCK
    rows = jnp.zeros((n_rows, d), h.dtype).at[dest].set(h[token_s])
    block_expert = jnp.minimum(
        jnp.searchsorted(padded_end, jnp.arange(n_blocks) * MOE_BLOCK, side='right'), N_EXPERTS - 1)

    def expert_block(args):
        xb, e = args
        a, g = jnp.split(xb @ w13[e], 2, axis=-1)
        return (jax.nn.silu(a) * g) @ w2[e]

    out = lax.map(expert_block, (rows.reshape(n_blocks, MOE_BLOCK, d), block_expert))
    y = out.reshape(n_rows, d)[dest] * w_s[:, None]
    return jnp.zeros_like(h).at[token_s].add(y)


def setup_inputs(seed: int = 0) -> dict:
    key = jax.random.key(seed)
    ks = jax.random.split(key, 32)
    L, D = DEPTH, D_MODEL

    def nrm(k, shape, scale):
        return jax.random.normal(k, shape, jnp.float32) * scale

    return {
        'x': nrm(ks[0], (BATCH, SEQ, D), 1.0),
        'c': nrm(ks[1], (BATCH, D), 1.0),
        'ctx': nrm(ks[2], (BATCH, CTX_LEN, D), 1.0),
        'c_ctx': nrm(ks[3], (D,), 1.0),
        'w_mod': nrm(ks[4], (L, D, N_MOD * D), D ** -0.5),
        'b_mod': nrm(ks[5], (L, N_MOD * D), 0.02),
        'norm_g': 1.0 + nrm(ks[6], (L, 4, D), 0.02),
        'w_in': nrm(ks[7], (L, D, IN_DIM), D ** -0.5),
        'b_gate': nrm(ks[8], (L, N_BRANCH * D), 0.02),
        'conv_dw': nrm(ks[9], (L, CONV_K, CONV_DIM), CONV_K ** -0.5),
        'conv_db': nrm(ks[10], (L, CONV_DIM), 0.02),
        'conv_ln_g': 1.0 + nrm(ks[11], (L, CONV_DIM), 0.02),
        'conv_ln_b': nrm(ks[12], (L, CONV_DIM), 0.02),
        'w_conv_out': nrm(ks[13], (L, CONV_DIM, D), CONV_DIM ** -0.5),
        'q_norm_g': 1.0 + nrm(ks[14], (L, Q_LORA), 0.02),
        'w_uq': nrm(ks[15], (L, Q_LORA, MLA_HEADS * QK_HEAD), Q_LORA ** -0.5),
        'kv_norm_g': 1.0 + nrm(ks[16], (L, KV_LORA), 0.02),
        'w_ukv': nrm(ks[17], (L, KV_LORA, MLA_HEADS * (QK_NOPE + V_HEAD)), KV_LORA ** -0.5),
        'w_attn_out': nrm(ks[18], (L, MLA_HEADS * V_HEAD, D), (MLA_HEADS * V_HEAD) ** -0.5),
        'gmlp_ln_g': 1.0 + nrm(ks[19], (L, GMLP_DIM), 0.02),
        'gmlp_ln_b': nrm(ks[20], (L, GMLP_DIM), 0.02),
        'w_spatial': nrm(ks[21], (L, GMLP_GROUPS, CHUNK, CHUNK), CHUNK ** -0.5),
        'b_spatial': 1.0 + nrm(ks[22], (L, GMLP_GROUPS, CHUNK), 0.02),
        'w_gmlp_out': nrm(ks[23], (L, GMLP_DIM, D), GMLP_DIM ** -0.5),
        'w_out': nrm(ks[24], (L, D, D), D ** -0.5),
        'ffn_w13': nrm(ks[25], (N_DENSE_LAYERS, D, 2 * D_FF), D ** -0.5),
        'ffn_w2': nrm(ks[26], (N_DENSE_LAYERS, D_FF, D), D_FF ** -0.5),
        'router_w': nrm(ks[27], (N_MOE_LAYERS, D, N_EXPERTS), D ** -0.5),
        'moe_w13': nrm(ks[28], (N_MOE_LAYERS, N_EXPERTS, D, 2 * D_FF_EXPERT), D ** -0.5),
        'moe_w2': nrm(ks[29], (N_MOE_LAYERS, N_EXPERTS, D_FF_EXPERT, D), D_FF_EXPERT ** -0.5),
    }


def reference(x, c, ctx, c_ctx, w_mod, b_mod, norm_g, w_in, b_gate, conv_dw, conv_db, conv_ln_g,
              conv_ln_b, w_conv_out, q_norm_g, w_uq, kv_norm_g, w_ukv, w_attn_out, gmlp_ln_g,
              gmlp_ln_b, w_spatial, b_spatial, w_gmlp_out, w_out, ffn_w13, ffn_w2, router_w,
              moe_w13, moe_w2):
    bsz, n_lat, d = x.shape
    rows = n_lat // GRID_W
    angles = axial_angles(rows)
    h_lat, h_ctx = x, ctx
    for layer in range(DEPTH):
        need_ctx = layer < DEPTH - 1
        mod_lat = (jax.nn.silu(c) @ w_mod[layer] + b_mod[layer])[:, None, :]
        mod_ctx = jax.nn.silu(c_ctx) @ w_mod[layer] + b_mod[layer]
        sh_t, sc_t, gt_t, sh_f, sc_f, gt_f = jnp.split(mod_lat, N_MOD, axis=-1)
        csh_t, csc_t, cgt_t, csh_f, csc_f, cgt_f = jnp.split(mod_ctx, N_MOD, axis=-1)
        g_pre_t, g_post_t = norm_g[layer, 0], norm_g[layer, 1]
        g_pre_f, g_post_f = norm_g[layer, 2], norm_g[layer, 3]

        a_lat = modulate(rms_norm(h_lat, g_pre_t), sh_t, sc_t)
        a_ctx = modulate(rms_norm(h_ctx, g_pre_t), csh_t, csc_t)
        y_lat, y_ctx = token_mixer(
            a_lat, a_ctx, angles, need_ctx, w_in[layer], b_gate[layer], conv_dw[layer],
            conv_db[layer], conv_ln_g[layer], conv_ln_b[layer], w_conv_out[layer],
            q_norm_g[layer], w_uq[layer], kv_norm_g[layer], w_ukv[layer], w_attn_out[layer],
            gmlp_ln_g[layer], gmlp_ln_b[layer], w_spatial[layer], b_spatial[layer],
            w_gmlp_out[layer], w_out[layer])
        h_lat = h_lat + gt_t * rms_norm(y_lat, g_post_t)
        if need_ctx:
            h_ctx = h_ctx + cgt_t * rms_norm(y_ctx, g_post_t)

        f_lat_in = modulate(rms_norm(h_lat, g_pre_f), sh_f, sc_f)
        if need_ctx:
            f_ctx_in = modulate(rms_norm(h_ctx, g_pre_f), csh_f, csc_f)
            tokens = jnp.concatenate([f_lat_in.reshape(-1, d), f_ctx_in.reshape(-1, d)], axis=0)
        else:
            tokens = f_lat_in.reshape(-1, d)
        idx = layer // 2
        if layer % 2 == 0:
            f = swiglu(tokens, ffn_w13[idx], ffn_w2[idx])
        else:
            f = moe_swiglu(tokens, router_w[idx], moe_w13[idx], moe_w2[idx])
        f_lat = f[: bsz * n_lat].reshape(bsz, n_lat, d)
        h_lat = h_lat + gt_f * rms_norm(f_lat, g_post_f)
        if need_ctx:
            f_ctx = f[bsz * n_lat:].reshape(bsz, -1, d)
            h_ctx = h_ctx + cgt_f * rms_norm(f_ctx, g_post_f)
    return h_lat
```

```python
import functools
import math

import jax
import jax.numpy as jnp
from jax import lax
from jax.experimental import pallas as pl
from jax.experimental.pallas import tpu as pltpu

F32 = jnp.float32
BF16 = jnp.bfloat16

GRID_W = 64
MLA_HEADS = 8
QK_NOPE = 64
QK_ROPE = 32
V_HEAD = 64
ROPE_THETA = 10000.0
CHUNK = 128
TOP_K = 2
N_MOD = 6
RMS_EPS = 1e-6
LN_EPS = 1e-5

LANES = 128
BF16_SUBLANES = 16
VMEM_LIMIT_BYTES = 56 * 1024 * 1024

HEAD_PAD = LANES
HEADS_PER_STEP = LANES // V_HEAD
TOKEN_TILE = 256
HALO = BF16_SUBLANES
Q_TILE = 512
KV_TILE = 768
MOE_ROWS = 512
GATHER_ROWS = 256
MOD_ROWS = 16


def _params(*sem):
    return pltpu.CompilerParams(dimension_semantics=sem, vmem_limit_bytes=VMEM_LIMIT_BYTES)


def _resident(shape):
    zeros = (0,) * len(shape)
    return pl.BlockSpec(shape, lambda *_: zeros, pipeline_mode=pl.Buffered(1))


def _dot(a, b):
    return jnp.dot(a, b, preferred_element_type=F32)


def _split_bf16(x):
    hi = x.astype(BF16)
    lo = (x - hi.astype(F32)).astype(BF16)
    return hi, lo


def _dot_f32(x, w_hi, w_lo):
    x_hi, x_lo = _split_bf16(x)
    return _dot(x_hi, w_hi) + (_dot(x_lo, w_hi) + _dot(x_hi, w_lo))


def _rms(x, g):
    return x * lax.rsqrt(jnp.mean(x * x, axis=-1, keepdims=True) + RMS_EPS) * g


def _layer_norm(x, g, b):
    xc = x - jnp.mean(x, axis=-1, keepdims=True)
    var = jnp.mean(xc * xc, axis=-1, keepdims=True)
    return xc * lax.rsqrt(var + LN_EPS) * g + b


def _silu(x):
    return x * jax.nn.sigmoid(x)


def _mod_kernel(c_ref, w_ref, b_ref, o_ref):
    w_hi, w_lo = _split_bf16(w_ref[0])
    o_ref[0] = _dot_f32(_silu(c_ref[...]), w_hi, w_lo) + b_ref[0]


def _modulation(c_rows, w_mod, b_mod):
    n_layers, d, n_out = w_mod.shape
    return pl.pallas_call(
        _mod_kernel,
        grid=(n_layers, n_out // d),
        in_specs=[
            pl.BlockSpec((MOD_ROWS, d), lambda l, j: (0, 0)),
            pl.BlockSpec((1, d, d), lambda l, j: (l, 0, j)),
            pl.BlockSpec((1, 1, d), lambda l, j: (l, 0, j)),
        ],
        out_specs=pl.BlockSpec((1, MOD_ROWS, d), lambda l, j: (l, 0, j)),
        out_shape=jax.ShapeDtypeStruct((n_layers, MOD_ROWS, n_out), F32),
        compiler_params=_params("arbitrary", "arbitrary"),
        name="modulation",
    )(c_rows, w_mod, b_mod.reshape(n_layers, 1, n_out))


def _inproj_kernel(h_ref, mod_ref, gpre_ref, cos_ref, sin_ref,
                   w_conv_ref, w_cq_ref, w_ckv_ref, w_kr_ref, w_gm_ref, w_gate_ref, b_gate_ref,
                   qg_ref, wuq_ref, wuq_rot_ref, kvg_ref, wk_ref, wv_ref, gln_g_ref, gln_b_ref,
                   u_ref, q_ref, k_ref, v_ref, gmu_ref, gmv_ref, gate_ref, *, scale):
    mod = mod_ref[0, 0]
    a = (_rms(h_ref[0], gpre_ref[...]) * (1.0 + mod[1:2]) + mod[0:1]).astype(BF16)
    cos = cos_ref[...]
    sin = sin_ref[...]

    zc = _dot(a, w_conv_ref[...])
    half = zc.shape[1] // 2
    u_ref[0] = (zc[:, :half] * jax.nn.sigmoid(zc[:, half:])).astype(BF16)

    cq = _rms(_dot(a, w_cq_ref[...]), qg_ref[...]).astype(BF16)
    qf = _dot(cq, wuq_ref[...])
    qr = _dot(cq, wuq_rot_ref[...])
    for hd in range(qf.shape[1] // HEAD_PAD):
        sl = slice(hd * HEAD_PAD, (hd + 1) * HEAD_PAD)
        q_ref[0, :, sl] = ((qf[:, sl] * cos + qr[:, sl] * sin) * scale).astype(BF16)

    ckv = _rms(_dot(a, w_ckv_ref[...]), kvg_ref[...]).astype(BF16)
    kr = _dot(a, w_kr_ref[...])
    kr = kr[:, :HEAD_PAD] * cos + kr[:, HEAD_PAD:] * sin
    kn = _dot(ckv, wk_ref[...])
    for hd in range(kn.shape[1] // HEAD_PAD):
        sl = slice(hd * HEAD_PAD, (hd + 1) * HEAD_PAD)
        k_ref[0, :, sl] = (kn[:, sl] + kr).astype(BF16)
    v_ref[0] = _dot(ckv, wv_ref[...]).astype(BF16)

    zg = jax.nn.gelu(_dot(a, w_gm_ref[...]), approximate=True)
    gd = zg.shape[1] // 2
    gmu_ref[0] = zg[:, :gd].astype(BF16)
    gmv_ref[0] = _layer_norm(zg[:, gd:], gln_g_ref[...], gln_b_ref[...]).astype(BF16)

    gate_ref[0] = jax.nn.sigmoid(_dot(a, w_gate_ref[...]) + b_gate_ref[...]).astype(BF16)


def _inproj(h, modtab, g_pre, cos_t, sin_t, wts, n_lat_tiles):
    bsz, nt, d = h.shape
    tm = TOKEN_TILE
    tok = lambda w: pl.BlockSpec((1, tm, w), lambda b, t: (b, t, 0))
    weights = [wts[k] for k in ("w_conv", "w_cq", "w_ckv", "w_kr", "w_gm", "w_gate", "b_gate", "q_norm_g",
                                "w_uq", "w_uq_rot", "kv_norm_g", "w_k", "w_v", "gmlp_ln_g", "gmlp_ln_b")]
    widths = (wts["w_conv"].shape[1] // 2, wts["w_uq"].shape[1], wts["w_k"].shape[1], wts["w_v"].shape[1],
              wts["w_gm"].shape[1] // 2, wts["w_gm"].shape[1] // 2, wts["w_gate"].shape[1])
    scale = 1.0 / math.sqrt(QK_NOPE + QK_ROPE)
    return pl.pallas_call(
        functools.partial(_inproj_kernel, scale=scale),
        grid=(bsz, nt // tm),
        in_specs=[
            tok(d),
            pl.BlockSpec((1, 1, N_MOD, d), lambda b, t: (b, jnp.where(t >= n_lat_tiles, 1, 0), 0, 0)),
            _resident(g_pre.shape),
            pl.BlockSpec((tm, LANES), lambda b, t: (t, 0)),
            pl.BlockSpec((tm, LANES), lambda b, t: (t, 0)),
        ] + [_resident(w.shape) for w in weights],
        out_specs=[tok(w) for w in widths],
        out_shape=[jax.ShapeDtypeStruct((bsz, nt, w), BF16) for w in widths],
        compiler_params=_params("parallel", "parallel"),
        name="inproj",
    )(h, modtab, g_pre, cos_t, sin_t, *weights)


def _attn_kernel(q_ref, k_ref, v_ref, o_ref, *, kv_tile):
    tq = q_ref.shape[1]
    n_kv = k_ref.shape[1] // kv_tile
    outs = []
    for hd in range(HEADS_PER_STEP):
        hs = slice(hd * HEAD_PAD, (hd + 1) * HEAD_PAD)
        q = q_ref[0, :, hs]

        def body(j, carry, hs=hs, q=q):
            m, l, acc = carry
            rows = pl.ds(pl.multiple_of(j * kv_tile, kv_tile), kv_tile)
            s = lax.dot_general(q, k_ref[0, rows, hs], (((1,), (1,)), ((), ())),
                                preferred_element_type=F32)
            m_new = jnp.maximum(m, jnp.max(s, axis=-1, keepdims=True))
            alpha = jnp.exp(m - m_new)
            p = jnp.exp(s - m_new)
            l = alpha * l + jnp.sum(p, axis=-1, keepdims=True)
            acc = alpha * acc + _dot(p.astype(BF16), v_ref[0, rows, :])
            return m_new, l, acc

        init = (jnp.full((tq, 1), -1e30, F32), jnp.zeros((tq, 1), F32), jnp.zeros((tq, LANES), F32))
        _, l, acc = lax.fori_loop(0, n_kv, body, init)
        outs.append(acc / l)
    lane = lax.broadcasted_iota(jnp.int32, (tq, LANES), 1)
    o = outs[-1]
    for hd in range(HEADS_PER_STEP - 2, -1, -1):
        o = jnp.where(lane < (hd + 1) * V_HEAD, outs[hd], o)
    o_ref[0] = o.astype(BF16)


def _attention(q, k, v, q_start, q_len, kv_start, kv_len):
    bsz = q.shape[0]
    n_pairs = v.shape[2] // LANES
    tq = min(Q_TILE, q_len)
    kv_tile = KV_TILE if kv_len % KV_TILE == 0 else math.gcd(kv_len, KV_TILE)
    q_off, kv_blk = q_start // tq, kv_start // kv_len
    qw = HEADS_PER_STEP * HEAD_PAD
    return pl.pallas_call(
        functools.partial(_attn_kernel, kv_tile=kv_tile),
        grid=(bsz, n_pairs, q_len // tq),
        in_specs=[
            pl.BlockSpec((1, tq, qw), lambda b, p, i: (b, q_off + i, p)),
            pl.BlockSpec((1, kv_len, qw), lambda b, p, i: (b, kv_blk, p)),
            pl.BlockSpec((1, kv_len, LANES), lambda b, p, i: (b, kv_blk, p)),
        ],
        out_specs=pl.BlockSpec((1, tq, LANES), lambda b, p, i: (b, i, p)),
        out_shape=jax.ShapeDtypeStruct((bsz, q_len, v.shape[2]), BF16),
        compiler_params=_params("parallel", "parallel", "arbitrary"),
        name="attention",
    )(q, k, v)


def _mixer_kernel(*refs, n_lat_tiles, n_tiles, has_ctx, moe_next, n_experts):
    (h_ref, mod_ref, u_ref, up_ref, un_ref, ol_ref, oc_ref, gmu_ref, gmv_ref, gate_ref,
     dw_ref, db_ref, cln_g_ref, cln_b_ref, wco_ref, wao_ref, ws_ref, bs_ref, wgo_ref, wout_ref,
     gpost_ref, gpre_ref) = refs[:22]
    rest = refs[22:]
    if moe_next:
        wr_hi_ref, wr_lo_ref, hout_ref, fin_ref, route_ref, win_ref = rest
    else:
        hout_ref, fin_ref, win_ref = rest
    t = pl.program_id(1)
    tm = h_ref.shape[1]
    d = h_ref.shape[2]
    mod = mod_ref[0, 0]

    prev_ok = jnp.logical_and(t != 0, t != n_lat_tiles)
    next_ok = jnp.logical_and(t != n_lat_tiles - 1, t != n_tiles - 1)
    win_ref[0:HALO, :] = jnp.where(prev_ok, up_ref[0].astype(F32), 0.0)
    win_ref[HALO:HALO + tm, :] = u_ref[0].astype(F32)
    win_ref[HALO + tm:, :] = jnp.where(next_ok, un_ref[0].astype(F32), 0.0)
    n_taps = dw_ref.shape[0]
    first = HALO - n_taps // 2
    chunks = []
    for c in range(dw_ref.shape[1] // LANES):
        cs = slice(c * LANES, (c + 1) * LANES)
        acc = jnp.zeros((tm, LANES), F32)
        for kk in range(n_taps):
            acc = acc + win_ref[first + kk:first + kk + tm, cs] * dw_ref[kk:kk + 1, cs]
        chunks.append(acc)
    conv = jnp.concatenate(chunks, axis=-1) + db_ref[...]
    c_act = _silu(_layer_norm(conv, cln_g_ref[...], cln_b_ref[...])).astype(BF16)
    br_conv = _dot(c_act, wco_ref[...])

    o = ol_ref[0]
    if has_ctx:
        o = jnp.where(t >= n_lat_tiles, oc_ref[0], o)
    br_attn = _dot(o, wao_ref[...])

    gmv = gmv_ref[0]
    lane = lax.broadcasted_iota(jnp.int32, (CHUNK, LANES), 1)
    group_dim = gmv.shape[1] // ws_ref.shape[0]
    rows = []
    for j in range(tm // CHUNK):
        cols = []
        for gp in range(gmv.shape[1] // LANES):
            vp = gmv[j * CHUNK:(j + 1) * CHUNK, gp * LANES:(gp + 1) * LANES]
            s0 = _dot(ws_ref[2 * gp], vp)
            s1 = _dot(ws_ref[2 * gp + 1], vp)
            cols.append(jnp.where(lane < group_dim, s0, s1))
        rows.append(jnp.concatenate(cols, axis=-1) + bs_ref[...])
    gm = (gmu_ref[0].astype(F32) * jnp.concatenate(rows, axis=0)).astype(BF16)
    br_gmlp = _dot(gm, wgo_ref[...])

    g = gate_ref[0]
    mix = (g[:, :d].astype(F32) * br_conv + g[:, d:2 * d].astype(F32) * br_attn
           + g[:, 2 * d:].astype(F32) * br_gmlp)
    y = _dot(mix.astype(BF16), wout_ref[...])
    h_new = h_ref[0] + mod[2:3] * _rms(y, gpost_ref[...])
    hout_ref[0] = h_new
    f_in = _rms(h_new, gpre_ref[...]) * (1.0 + mod[4:5]) + mod[3:4]
    fin_ref[0] = f_in.astype(fin_ref.dtype)

    if moe_next:
        logits = _dot_f32(f_in, wr_hi_ref[...], wr_lo_ref[...])
        ln = lax.broadcasted_iota(jnp.int32, logits.shape, 1)
        neg = -jnp.inf
        lg = jnp.where(ln < n_experts, logits, neg)
        m1 = jnp.max(lg, axis=-1, keepdims=True)
        i1 = jnp.min(jnp.where(lg == m1, ln, LANES), axis=-1, keepdims=True)
        lg2 = jnp.where(ln == i1, neg, lg)
        m2 = jnp.max(lg2, axis=-1, keepdims=True)
        i2 = jnp.min(jnp.where(lg2 == m2, ln, LANES), axis=-1, keepdims=True)
        e2 = jnp.exp(m2 - m1)
        w1 = 1.0 / (1.0 + e2)
        w2 = e2 / (1.0 + e2)
        route = jnp.where(ln == 0, i1.astype(F32),
                          jnp.where(ln == 1, i2.astype(F32),
                                    jnp.where(ln == 2, w1, jnp.where(ln == 3, w2, 0.0))))
        route_ref[0] = route


def _mixer(h, modtab, u, o_lat, o_ctx, gmu, gmv, gates, wts, g_post, g_pre_f, n_lat_tiles, rows_out,
           router):
    bsz, nt, d = h.shape
    tm = TOKEN_TILE
    n_tiles = nt // tm
    has_ctx = o_ctx is not None
    moe_next = router is not None
    if not has_ctx:
        o_ctx = o_lat
    n_ctx_blocks = o_ctx.shape[1] // tm
    hb = tm // HALO
    last_halo = nt // HALO - 1
    tok = lambda w: pl.BlockSpec((1, tm, w), lambda b, t: (b, t, 0))
    cd = u.shape[2]
    weights = [wts[k] for k in ("conv_dw", "conv_db", "conv_ln_g", "conv_ln_b", "w_conv_out", "w_attn_out",
                                "w_spatial", "b_spatial", "w_gmlp_out", "w_out")] + [g_post, g_pre_f]
    in_specs = [
        tok(d),
        pl.BlockSpec((1, 1, N_MOD, d), lambda b, t: (b, jnp.where(t >= n_lat_tiles, 1, 0), 0, 0)),
        tok(cd),
        pl.BlockSpec((1, HALO, cd), lambda b, t: (b, jnp.maximum(t * hb - 1, 0), 0)),
        pl.BlockSpec((1, HALO, cd), lambda b, t: (b, jnp.minimum((t + 1) * hb, last_halo), 0)),
        pl.BlockSpec((1, tm, o_lat.shape[2]), lambda b, t: (b, jnp.minimum(t, n_lat_tiles - 1), 0)),
        pl.BlockSpec((1, tm, o_ctx.shape[2]),
                     lambda b, t: (b, jnp.clip(t - n_lat_tiles, 0, n_ctx_blocks - 1), 0)),
        tok(gmu.shape[2]), tok(gmv.shape[2]), tok(gates.shape[2]),
    ] + [_resident(w.shape) for w in weights]
    operands = [h, modtab, u, u, u, o_lat, o_ctx, gmu, gmv, gates] + weights
    out_specs = [tok(d), tok(d)]
    out_shape = [jax.ShapeDtypeStruct((bsz, rows_out, d), F32),
                 jax.ShapeDtypeStruct((bsz, rows_out, d), F32 if moe_next else BF16)]
    n_experts = 0
    if moe_next:
        n_experts = router[2]
        in_specs += [_resident(router[0].shape), _resident(router[1].shape)]
        operands += [router[0], router[1]]
        out_specs.append(tok(LANES))
        out_shape.append(jax.ShapeDtypeStruct((bsz, rows_out, LANES), F32))
    return pl.pallas_call(
        functools.partial(_mixer_kernel, n_lat_tiles=n_lat_tiles, n_tiles=n_tiles, has_ctx=has_ctx,
                          moe_next=moe_next, n_experts=n_experts),
        grid=(bsz, rows_out // tm),
        in_specs=in_specs,
        out_specs=out_specs,
        out_shape=out_shape,
        scratch_shapes=[pltpu.VMEM((tm + 2 * HALO, cd), F32)],
        compiler_params=_params("parallel", "parallel"),
        name="mixer",
    )(*operands)


def _swiglu_block(x, w13_ref, w2_ref, ff_tile):
    d_ff = w2_ref.shape[1]
    acc = jnp.zeros((x.shape[0], w2_ref.shape[2]), F32)
    for j in range(d_ff // ff_tile):
        a = _dot(x, w13_ref[0, :, j * ff_tile:(j + 1) * ff_tile])
        g = _dot(x, w13_ref[0, :, d_ff + j * ff_tile:d_ff + (j + 1) * ff_tile])
        acc = acc + _dot((_silu(a) * g).astype(BF16), w2_ref[0, j * ff_tile:(j + 1) * ff_tile, :])
    return acc


def _dense_ffn_kernel(x_ref, h_ref, mod_ref, gpost_ref, w13_ref, w2_ref, o_ref, *, ff_tile):
    f = _swiglu_block(x_ref[0], w13_ref, w2_ref, ff_tile)
    o_ref[0] = h_ref[0] + mod_ref[0, 0][5:6] * _rms(f, gpost_ref[...])


def _ff_tile(d_ff, target):
    n = d_ff // LANES
    best = 1
    for c in range(1, n + 1):
        if n % c == 0 and c * LANES <= target:
            best = c
    return best * LANES


def _dense_ffn(f_in, h, modtab, g_post, w13, w2, n_lat_tiles):
    bsz, rows, d = h.shape
    tm = TOKEN_TILE
    tok = lambda w: pl.BlockSpec((1, tm, w), lambda b, t: (b, t, 0))
    return pl.pallas_call(
        functools.partial(_dense_ffn_kernel, ff_tile=_ff_tile(w2.shape[1], 1536)),
        grid=(bsz, rows // tm),
        in_specs=[
            tok(d), tok(d),
            pl.BlockSpec((1, 1, N_MOD, d), lambda b, t: (b, jnp.where(t >= n_lat_tiles, 1, 0), 0, 0)),
            _resident(g_post.shape), _resident(w13.shape), _resident(w2.shape),
        ],
        out_specs=tok(d),
        out_shape=jax.ShapeDtypeStruct((bsz, rows, d), F32),
        compiler_params=_params("parallel", "parallel"),
        name="dense_ffn",
    )(f_in, h, modtab, g_post, w13, w2)


def _gather_kernel(idx_ref, src_ref, out_ref, sem, *, rows):
    base = pl.program_id(0) * rows

    def copy(r, src_row):
        return pltpu.make_async_copy(src_ref.at[pl.ds(src_row, 1)], out_ref.at[pl.ds(base + r, 1)], sem)

    def issue(r, carry):
        copy(r, idx_ref[0, 0, r]).start()
        return carry

    def drain(r, carry):
        copy(r, 0).wait()
        return carry

    lax.fori_loop(0, rows, issue, 0)
    lax.fori_loop(0, rows, drain, 0)


def _gather_rows(src, idx):
    n = idx.shape[0]
    rows = GATHER_ROWS
    return pl.pallas_call(
        functools.partial(_gather_kernel, rows=rows),
        grid=(n // rows,),
        in_specs=[
            pl.BlockSpec((1, 1, rows), lambda i: (i, 0, 0), memory_space=pltpu.SMEM),
            pl.BlockSpec(memory_space=pl.ANY),
        ],
        out_specs=pl.BlockSpec(memory_space=pl.ANY),
        out_shape=jax.ShapeDtypeStruct((n, src.shape[1]), src.dtype),
        scratch_shapes=[pltpu.SemaphoreType.DMA(())],
        compiler_params=_params("arbitrary"),
        name="gather_rows",
    )(idx.reshape(n // rows, 1, rows), src)


def _expert_ffn_kernel(be_ref, x_ref, w13_ref, w2_ref, o_ref, *, ff_tile):
    del be_ref
    o_ref[...] = _swiglu_block(x_ref[...].astype(BF16), w13_ref, w2_ref, ff_tile)


def _expert_ffn(x_rows, block_expert, w13, w2):
    n_rows, d = x_rows.shape
    rb = MOE_ROWS
    _, _, two_f = w13.shape
    d_ff = w2.shape[1]
    grid_spec = pltpu.PrefetchScalarGridSpec(
        num_scalar_prefetch=1,
        grid=(n_rows // rb,),
        in_specs=[
            pl.BlockSpec((rb, d), lambda i, be: (i, 0)),
            pl.BlockSpec((1, d, two_f), lambda i, be: (be[i], 0, 0), pipeline_mode=pl.Buffered(1)),
            pl.BlockSpec((1, d_ff, d), lambda i, be: (be[i], 0, 0), pipeline_mode=pl.Buffered(1)),
        ],
        out_specs=pl.BlockSpec((rb, d), lambda i, be: (i, 0)),
    )
    return pl.pallas_call(
        functools.partial(_expert_ffn_kernel, ff_tile=_ff_tile(d_ff, 512)),
        grid_spec=grid_spec,
        out_shape=jax.ShapeDtypeStruct((n_rows, d), F32),
        compiler_params=_params("arbitrary"),
        name="expert_ffn",
    )(block_expert, x_rows, w13, w2)


def _combine_kernel(y0_ref, y1_ref, route_ref, h_ref, mod_ref, gpost_ref, o_ref):
    route = route_ref[0]
    f = route[:, 2:3] * y0_ref[0, 0] + route[:, 3:4] * y1_ref[0, 0]
    o_ref[0] = h_ref[0] + mod_ref[0, 0][5:6] * _rms(f, gpost_ref[...])


def _combine(y2, route, h, modtab, g_post, n_lat_tiles):
    bsz, rows, d = h.shape
    tm = TOKEN_TILE
    tpb = rows // tm
    tok = lambda w: pl.BlockSpec((1, tm, w), lambda b, t: (b, t, 0))
    return pl.pallas_call(
        _combine_kernel,
        grid=(bsz, tpb),
        in_specs=[
            pl.BlockSpec((1, 1, tm, d), lambda b, t: (0, b * tpb + t, 0, 0)),
            pl.BlockSpec((1, 1, tm, d), lambda b, t: (1, b * tpb + t, 0, 0)),
            tok(LANES), tok(d),
            pl.BlockSpec((1, 1, N_MOD, d), lambda b, t: (b, jnp.where(t >= n_lat_tiles, 1, 0), 0, 0)),
            _resident(g_post.shape),
        ],
        out_specs=tok(d),
        out_shape=jax.ShapeDtypeStruct((bsz, rows, d), F32),
        compiler_params=_params("parallel", "parallel"),
        name="moe_combine",
    )(y2, y2, route, h, modtab, g_post)


def _moe_ffn(f_in, route, h, modtab, g_post, w13, w2, n_lat_tiles):
    bsz, rows, d = h.shape
    n_tok = bsz * rows
    n_exp = w13.shape[0]
    n_assign = n_tok * TOP_K
    rb = MOE_ROWS
    expert = route[..., :TOP_K].astype(jnp.int32).reshape(n_assign)
    onehot = (expert[:, None] == jnp.arange(n_exp, dtype=jnp.int32)[None, :]).astype(jnp.int32)
    csum = jnp.cumsum(onehot, axis=0)
    rank = jnp.sum(csum * onehot, axis=1) - 1
    counts = csum[-1]
    padded = (counts + rb - 1) // rb * rb
    padded_end = jnp.cumsum(padded)
    dest = (padded_end - padded)[expert] + rank
    n_rows = n_assign + n_exp * rb
    token = jnp.arange(n_assign, dtype=jnp.int32) // TOP_K
    src = jnp.zeros((n_rows,), jnp.int32).at[dest].set(token)
    block_expert = jnp.minimum(
        jnp.searchsorted(padded_end, jnp.arange(n_rows // rb, dtype=jnp.int32) * rb, side="right"),
        n_exp - 1).astype(jnp.int32)

    x_rows = _gather_rows(f_in.reshape(n_tok, d), src)
    y_rows = _expert_ffn(x_rows, block_expert, w13, w2)
    back = dest.reshape(n_tok, TOP_K).T.reshape(n_assign)
    y2 = _gather_rows(y_rows, back).reshape(TOP_K, n_tok // TOKEN_TILE, TOKEN_TILE, d)
    return _combine(y2, route, h, modtab, g_post, n_lat_tiles)


def _rot_cols(r):
    q = QK_ROPE // 4
    return jnp.concatenate([-r[..., q:2 * q], r[..., :q], -r[..., 3 * q:], r[..., 2 * q:3 * q]], axis=-1)


def _head_pad(nope, rope):
    pad = jnp.zeros(nope.shape[:-1] + (HEAD_PAD - QK_NOPE - QK_ROPE,), nope.dtype)
    out = jnp.concatenate([nope, rope, pad], axis=-1)
    return out.reshape(out.shape[:-2] + (out.shape[-2] * HEAD_PAD,))


def _layer_weights(layer, w_in, b_gate, conv_dw, conv_db, conv_ln_g, conv_ln_b, w_conv_out, q_norm_g, w_uq,
                   kv_norm_g, w_ukv, w_attn_out, gmlp_ln_g, gmlp_ln_b, w_spatial, b_spatial, w_gmlp_out,
                   w_out):
    d = w_in.shape[1]
    conv_dim = conv_dw.shape[2]
    q_lora, kv_lora = w_uq.shape[1], w_ukv.shape[1]
    gmlp_dim = w_gmlp_out.shape[1]
    col_q = 2 * conv_dim
    col_kv = col_q + q_lora
    col_kr = col_kv + kv_lora
    col_gm = col_kr + QK_ROPE
    col_gate = col_gm + 2 * gmlp_dim
    wi = w_in[layer]
    row = lambda v: v[layer].reshape(1, -1)

    uq = w_uq[layer].reshape(q_lora, MLA_HEADS, QK_NOPE + QK_ROPE)
    uq_nope, uq_rope = uq[..., :QK_NOPE], uq[..., QK_NOPE:]
    ukv = w_ukv[layer].reshape(kv_lora, MLA_HEADS, QK_NOPE + V_HEAD)
    wkr = wi[:, col_kr:col_gm]
    zeros_nope = jnp.zeros((d, 1, QK_NOPE), F32)
    groups = w_spatial.shape[1]
    return {
        "w_conv": wi[:, :col_q].astype(BF16),
        "w_cq": wi[:, col_q:col_kv].astype(BF16),
        "w_ckv": wi[:, col_kv:col_kr].astype(BF16),
        "w_kr": jnp.concatenate([_head_pad(zeros_nope, wkr[:, None, :]),
                                 _head_pad(zeros_nope, _rot_cols(wkr)[:, None, :])], axis=1).astype(BF16),
        "w_gm": wi[:, col_gm:col_gate].astype(BF16),
        "w_gate": wi[:, col_gate:].astype(BF16),
        "b_gate": row(b_gate),
        "q_norm_g": row(q_norm_g),
        "w_uq": _head_pad(uq_nope, uq_rope).astype(BF16),
        "w_uq_rot": _head_pad(jnp.zeros_like(uq_nope), _rot_cols(uq_rope)).astype(BF16),
        "kv_norm_g": row(kv_norm_g),
        "w_k": _head_pad(ukv[..., :QK_NOPE], jnp.zeros((kv_lora, MLA_HEADS, QK_ROPE), F32)).astype(BF16),
        "w_v": ukv[..., QK_NOPE:].reshape(kv_lora, MLA_HEADS * V_HEAD).astype(BF16),
        "gmlp_ln_g": row(gmlp_ln_g),
        "gmlp_ln_b": row(gmlp_ln_b),
        "conv_dw": conv_dw[layer],
        "conv_db": row(conv_db),
        "conv_ln_g": row(conv_ln_g),
        "conv_ln_b": row(conv_ln_b),
        "w_conv_out": w_conv_out[layer].astype(BF16),
        "w_attn_out": w_attn_out[layer].astype(BF16),
        "w_spatial": w_spatial[layer].astype(BF16),
        "b_spatial": jnp.repeat(b_spatial[layer].T, gmlp_dim // groups, axis=1),
        "w_gmlp_out": w_gmlp_out[layer].astype(BF16),
        "w_out": w_out[layer].astype(BF16),
    }


def _rope_tables(n_lat, n_ctx):
    pairs = QK_ROPE // 4
    pos = jnp.arange(n_lat, dtype=jnp.int32)
    row = (pos // GRID_W).astype(F32)
    col = (pos % GRID_W).astype(F32)
    inv_freq = ROPE_THETA ** (-jnp.arange(pairs, dtype=F32) / pairs)
    ar, ac = row[:, None] * inv_freq, col[:, None] * inv_freq

    def table(fn, fill):
        rope = jnp.concatenate([fn(ar), fn(ar), fn(ac), fn(ac)], axis=-1)
        lat = jnp.concatenate([jnp.full((n_lat, QK_NOPE), fill, F32), rope,
                               jnp.full((n_lat, HEAD_PAD - QK_NOPE - QK_ROPE), fill, F32)], axis=-1)
        return jnp.concatenate([lat, jnp.full((n_ctx, HEAD_PAD), fill, F32)], axis=0)

    return table(jnp.cos, 1.0), table(jnp.sin, 0.0)


def kernel(x, c, ctx, c_ctx, w_mod, b_mod, norm_g, w_in, b_gate, conv_dw, conv_db, conv_ln_g, conv_ln_b,
           w_conv_out, q_norm_g, w_uq, kv_norm_g, w_ukv, w_attn_out, gmlp_ln_g, gmlp_ln_b, w_spatial,
           b_spatial, w_gmlp_out, w_out, ffn_w13, ffn_w2, router_w, moe_w13, moe_w2):
    bsz, n_lat, d = x.shape
    n_ctx = ctx.shape[1]
    depth = w_mod.shape[0]
    tm = TOKEN_TILE
    assert n_lat % Q_TILE == 0 and n_lat % n_ctx == 0 and n_ctx % tm == 0 and n_lat % GRID_W == 0
    assert bsz + 1 <= MOD_ROWS and w_spatial.shape[2] == CHUNK and V_HEAD * HEADS_PER_STEP == LANES
    n_lat_tiles = n_lat // tm

    c_rows = jnp.zeros((MOD_ROWS, d), F32).at[:bsz].set(c).at[bsz].set(c_ctx)
    mod_all = _modulation(c_rows, w_mod, b_mod)
    cos_t, sin_t = _rope_tables(n_lat, n_ctx)
    h = jnp.concatenate([x, ctx], axis=1)

    for layer in range(depth):
        need_ctx = layer < depth - 1
        rows_out = n_lat + n_ctx if need_ctx else n_lat
        mod = mod_all[layer].reshape(MOD_ROWS, N_MOD, d)
        modtab = jnp.stack([mod[:bsz], jnp.broadcast_to(mod[bsz], (bsz, N_MOD, d))], axis=1)
        g = norm_g[layer].reshape(4, 1, d)
        wts = _layer_weights(layer, w_in, b_gate, conv_dw, conv_db, conv_ln_g, conv_ln_b, w_conv_out,
                             q_norm_g, w_uq, kv_norm_g, w_ukv, w_attn_out, gmlp_ln_g, gmlp_ln_b, w_spatial,
                             b_spatial, w_gmlp_out, w_out)

        u, q, k, v, gmu, gmv, gates = _inproj(h, modtab, g[0], cos_t, sin_t, wts, n_lat_tiles)
        o_lat = _attention(q, k, v, 0, n_lat, 0, n_lat + n_ctx)
        o_ctx = _attention(q, k, v, n_lat, n_ctx, n_lat, n_ctx) if need_ctx else None

        idx = layer // 2
        router = None
        if layer % 2 == 1:
            n_exp = router_w.shape[2]
            wr = jnp.zeros((d, LANES), F32).at[:, :n_exp].set(router_w[idx])
            wr_hi = wr.astype(BF16)
            router = (wr_hi, (wr - wr_hi.astype(F32)).astype(BF16), n_exp)
        outs = _mixer(h, modtab, u, o_lat, o_ctx, gmu, gmv, gates, wts, g[1], g[2], n_lat_tiles, rows_out,
                      router)
        if router is None:
            h_mid, f_in = outs
            h = _dense_ffn(f_in, h_mid, modtab, g[3], ffn_w13[idx].astype(BF16)[None],
                           ffn_w2[idx].astype(BF16)[None], n_lat_tiles)
        else:
            h_mid, f_in, route = outs
            h = _moe_ffn(f_in, route, h_mid, modtab, g[3], moe_w13[idx].astype(BF16),
                         moe_w2[idx].astype(BF16), n_lat_tiles)
    return h[:, :n_lat]
```

```python
import functools
import math

import jax
import jax.numpy as jnp
from jax import lax
from jax.experimental import pallas as pl
from jax.experimental.pallas import tpu as pltpu

F32 = jnp.float32
BF16 = jnp.bfloat16

GRID_W = 64
MLA_HEADS = 8
QK_NOPE = 64
QK_ROPE = 32
V_HEAD = 64
ROPE_THETA = 10000.0
CHUNK = 128
TOP_K = 2
N_MOD = 6
RMS_EPS = 1e-6
LN_EPS = 1e-5

LANES = 128
F32_SUBLANES = 8
BF16_SUBLANES = 16
VMEM_LIMIT_BYTES = 56 * 1024 * 1024

HEAD_PAD = LANES
HEADS_PER_STEP = LANES // V_HEAD
TOKEN_TILE = 256
HALO = BF16_SUBLANES
Q_TILE = 512
KV_TILE = 256
MOE_ROWS = 512
MOD_ROWS = 16


def _params(*sem):
    return pltpu.CompilerParams(dimension_semantics=sem, vmem_limit_bytes=VMEM_LIMIT_BYTES)


def _resident(shape):
    zeros = (0,) * len(shape)
    return pl.BlockSpec(shape, lambda *_: zeros, pipeline_mode=pl.Buffered(1))


def _dot(a, b):
    return jnp.dot(a, b, preferred_element_type=F32)


def _split_bf16(x):
    hi = x.astype(BF16)
    lo = (x - hi.astype(F32)).astype(BF16)
    return hi, lo


def _dot_f32(x, w_hi, w_lo):
    x_hi, x_lo = _split_bf16(x)
    return _dot(x_hi, w_hi) + (_dot(x_lo, w_hi) + _dot(x_hi, w_lo))


def _rms(x, g):
    return x * lax.rsqrt(jnp.mean(x * x, axis=-1, keepdims=True) + RMS_EPS) * g


def _layer_norm(x, g, b):
    xc = x - jnp.mean(x, axis=-1, keepdims=True)
    var = jnp.mean(xc * xc, axis=-1, keepdims=True)
    return xc * lax.rsqrt(var + LN_EPS) * g + b


def _silu(x):
    return x * jax.nn.sigmoid(x)


def _load_row_tiles(ref, rows):
    return jnp.concatenate([ref[pl.ds(s, rows, stride=F32_SUBLANES), :] for s in range(F32_SUBLANES)], axis=1)


def _store_row_tiles(ref, x):
    for s in range(F32_SUBLANES):
        ref[pl.ds(s, x.shape[0], stride=F32_SUBLANES), :] = x[:, s * LANES:(s + 1) * LANES]


def _mod_kernel(c_ref, w_ref, b_ref, o_ref):
    w_hi, w_lo = _split_bf16(w_ref[0])
    o_ref[0] = _dot_f32(_silu(c_ref[...]), w_hi, w_lo) + b_ref[0]


def _modulation(c_rows, w_mod, b_mod):
    n_layers, d, n_out = w_mod.shape
    return pl.pallas_call(
        _mod_kernel,
        grid=(n_layers, n_out // d),
        in_specs=[
            pl.BlockSpec((MOD_ROWS, d), lambda l, j: (0, 0)),
            pl.BlockSpec((1, d, d), lambda l, j: (l, 0, j)),
            pl.BlockSpec((1, 1, d), lambda l, j: (l, 0, j)),
        ],
        out_specs=pl.BlockSpec((1, MOD_ROWS, d), lambda l, j: (l, 0, j)),
        out_shape=jax.ShapeDtypeStruct((n_layers, MOD_ROWS, n_out), F32),
        compiler_params=_params("arbitrary", "arbitrary"),
        name="modulation",
    )(c_rows, w_mod, b_mod.reshape(n_layers, 1, n_out))


def _inproj_kernel(h_ref, mod_ref, gpre_ref, cos_ref, sin_ref,
                   w_conv_ref, w_cq_ref, w_ckv_ref, w_kr_ref, w_gm_ref, w_gate_ref, b_gate_ref,
                   qg_ref, wuq_ref, wuq_rot_ref, kvg_ref, wk_ref, wv_ref, vone_ref, gln_g_ref, gln_b_ref,
                   u_ref, q_ref, k_ref, v_ref, gmu_ref, gmv_ref, gate_ref, *, scale):
    mod = mod_ref[0, 0]
    a = (_rms(h_ref[0], gpre_ref[...]) * (1.0 + mod[1:2]) + mod[0:1]).astype(BF16)
    cos = cos_ref[...]
    sin = sin_ref[...]

    zc = _dot(a, w_conv_ref[...])
    half = zc.shape[1] // 2
    u_ref[0] = (zc[:, :half] * jax.nn.sigmoid(zc[:, half:])).astype(BF16)

    cq = _rms(_dot(a, w_cq_ref[...]), qg_ref[...]).astype(BF16)
    qf = _dot(cq, wuq_ref[...])
    qr = _dot(cq, wuq_rot_ref[...])
    for hd in range(qf.shape[1] // HEAD_PAD):
        sl = slice(hd * HEAD_PAD, (hd + 1) * HEAD_PAD)
        q_ref[0, :, sl] = ((qf[:, sl] * cos + qr[:, sl] * sin) * scale).astype(BF16)

    ckv = _rms(_dot(a, w_ckv_ref[...]), kvg_ref[...]).astype(BF16)
    kr = _dot(a, w_kr_ref[...])
    kr = kr[:, :HEAD_PAD] * cos + kr[:, HEAD_PAD:] * sin
    kn = _dot(ckv, wk_ref[...])
    for hd in range(kn.shape[1] // HEAD_PAD):
        sl = slice(hd * HEAD_PAD, (hd + 1) * HEAD_PAD)
        k_ref[0, :, sl] = (kn[:, sl] + kr).astype(BF16)
    v_ref[0] = (_dot(ckv, wv_ref[...]) + vone_ref[...]).astype(BF16)

    zg = jax.nn.gelu(_dot(a, w_gm_ref[...]), approximate=True)
    gd = zg.shape[1] // 2
    gmu_ref[0] = zg[:, :gd].astype(BF16)
    gmv_ref[0] = _layer_norm(zg[:, gd:], gln_g_ref[...], gln_b_ref[...]).astype(BF16)

    gate_ref[0] = jax.nn.sigmoid(_dot(a, w_gate_ref[...]) + b_gate_ref[...]).astype(BF16)


def _inproj(h, modtab, g_pre, cos_t, sin_t, wts, n_lat_tiles):
    bsz, nt, d = h.shape
    tm = TOKEN_TILE
    tok = lambda w: pl.BlockSpec((1, tm, w), lambda b, t: (b, t, 0))
    weights = [wts[k] for k in ("w_conv", "w_cq", "w_ckv", "w_kr", "w_gm", "w_gate", "b_gate", "q_norm_g",
                                "w_uq", "w_uq_rot", "kv_norm_g", "w_k", "w_v", "v_one", "gmlp_ln_g",
                                "gmlp_ln_b")]
    widths = (wts["w_conv"].shape[1] // 2, wts["w_uq"].shape[1], wts["w_k"].shape[1], wts["w_v"].shape[1],
              wts["w_gm"].shape[1] // 2, wts["w_gm"].shape[1] // 2, wts["w_gate"].shape[1])
    scale = math.log2(math.e) / math.sqrt(QK_NOPE + QK_ROPE)
    return pl.pallas_call(
        functools.partial(_inproj_kernel, scale=scale),
        grid=(bsz, nt // tm),
        in_specs=[
            tok(d),
            pl.BlockSpec((1, 1, N_MOD, d), lambda b, t: (b, jnp.where(t >= n_lat_tiles, 1, 0), 0, 0)),
            _resident(g_pre.shape),
            pl.BlockSpec((tm, LANES), lambda b, t: (t, 0)),
            pl.BlockSpec((tm, LANES), lambda b, t: (t, 0)),
        ] + [_resident(w.shape) for w in weights],
        out_specs=[tok(w) for w in widths],
        out_shape=[jax.ShapeDtypeStruct((bsz, nt, w), BF16) for w in widths],
        compiler_params=_params("parallel", "parallel"),
        name="inproj",
    )(h, modtab, g_pre, cos_t, sin_t, *weights)


def _one_lane(head):
    return ((head % HEADS_PER_STEP + 1) % HEADS_PER_STEP) * V_HEAD


def _attn_kernel(q_ref, k_ref, v_ref, o_ref, *, kv_tile):
    tq = q_ref.shape[1]
    n_kv = k_ref.shape[1] // kv_tile
    heads = [slice(hd * HEAD_PAD, (hd + 1) * HEAD_PAD) for hd in range(HEADS_PER_STEP)]
    qs = [q_ref[0, :, hs] for hs in heads]

    def body(j, carry):
        rows = pl.ds(pl.multiple_of(j * kv_tile, kv_tile), kv_tile)
        out = []
        for hs, q, (m, acc) in zip(heads, qs, carry):
            s = lax.dot_general(q, k_ref[0, rows, hs], (((1,), (1,)), ((), ())),
                                preferred_element_type=F32)
            m_new = jnp.maximum(m, jnp.max(s, axis=-1, keepdims=True))
            p = jnp.exp2(s - m_new).astype(BF16)
            acc = jnp.exp2(m - m_new) * acc + _dot(p, v_ref[0, rows, hs])
            out.append((m_new, acc))
        return tuple(out)

    init = tuple((jnp.full((tq, 1), -1e30, F32), jnp.zeros((tq, LANES), F32)) for _ in heads)
    carry = lax.fori_loop(0, n_kv, body, init, unroll=True)
    lane = lax.broadcasted_iota(jnp.int32, (tq, LANES), 1)
    o = None
    for hd in range(HEADS_PER_STEP - 1, -1, -1):
        acc = carry[hd][1]
        one = _one_lane(hd)
        o_hd = acc / acc[:, one:one + 1]
        o = o_hd if o is None else jnp.where(lane < (hd + 1) * V_HEAD, o_hd, o)
    o_ref[0] = o.astype(BF16)


def _attention(q, k, v, q_start, q_len, kv_start, kv_len):
    bsz = q.shape[0]
    qw = HEADS_PER_STEP * HEAD_PAD
    n_pairs = v.shape[2] // qw
    tq = min(Q_TILE, q_len)
    kv_tile = KV_TILE if kv_len % KV_TILE == 0 else math.gcd(kv_len, KV_TILE)
    q_off, kv_blk = q_start // tq, kv_start // kv_len
    return pl.pallas_call(
        functools.partial(_attn_kernel, kv_tile=kv_tile),
        grid=(bsz, n_pairs, q_len // tq),
        in_specs=[
            pl.BlockSpec((1, tq, qw), lambda b, p, i: (b, q_off + i, p)),
            pl.BlockSpec((1, kv_len, qw), lambda b, p, i: (b, kv_blk, p)),
            pl.BlockSpec((1, kv_len, qw), lambda b, p, i: (b, kv_blk, p)),
        ],
        out_specs=pl.BlockSpec((1, tq, LANES), lambda b, p, i: (b, i, p)),
        out_shape=jax.ShapeDtypeStruct((bsz, q_len, n_pairs * LANES), BF16),
        compiler_params=_params("parallel", "parallel", "arbitrary"),
        name="attention",
    )(q, k, v)


def _mixer_kernel(*refs, n_lat_tiles, n_tiles, has_ctx, moe_next, n_experts):
    (h_ref, mod_ref, u_ref, up_ref, un_ref, ol_ref, oc_ref, gmu_ref, gmv_ref, gate_ref,
     dw_ref, db_ref, cln_g_ref, cln_b_ref, wco_ref, wao_ref, ws_ref, bs_ref, wgo_ref, wout_ref,
     gpost_ref, gpre_ref) = refs[:22]
    rest = refs[22:]
    if moe_next:
        wr_hi_ref, wr_lo_ref, hout_ref, fin_ref, route_ref, win_ref = rest
    else:
        hout_ref, fin_ref, win_ref = rest
    t = pl.program_id(1)
    tm = h_ref.shape[1]
    d = h_ref.shape[2]
    mod = mod_ref[0, 0]

    prev_ok = jnp.logical_and(t != 0, t != n_lat_tiles)
    next_ok = jnp.logical_and(t != n_lat_tiles - 1, t != n_tiles - 1)
    win_ref[0:HALO, :] = jnp.where(prev_ok, up_ref[0].astype(F32), 0.0)
    win_ref[HALO:HALO + tm, :] = u_ref[0].astype(F32)
    win_ref[HALO + tm:, :] = jnp.where(next_ok, un_ref[0].astype(F32), 0.0)
    n_taps = dw_ref.shape[0]
    first = HALO - n_taps // 2
    chunks = []
    for c in range(dw_ref.shape[1] // LANES):
        cs = slice(c * LANES, (c + 1) * LANES)
        acc = jnp.zeros((tm, LANES), F32)
        for kk in range(n_taps):
            acc = acc + win_ref[first + kk:first + kk + tm, cs] * dw_ref[kk:kk + 1, cs]
        chunks.append(acc)
    conv = jnp.concatenate(chunks, axis=-1) + db_ref[...]
    c_act = _silu(_layer_norm(conv, cln_g_ref[...], cln_b_ref[...])).astype(BF16)
    br_conv = _dot(c_act, wco_ref[...])

    o = ol_ref[0]
    if has_ctx:
        o = jnp.where(t >= n_lat_tiles, oc_ref[0], o)
    br_attn = _dot(o, wao_ref[...])

    gmv = gmv_ref[0]
    lane = lax.broadcasted_iota(jnp.int32, (CHUNK, LANES), 1)
    group_dim = gmv.shape[1] // ws_ref.shape[0]
    rows = []
    for j in range(tm // CHUNK):
        cols = []
        for gp in range(gmv.shape[1] // LANES):
            vp = gmv[j * CHUNK:(j + 1) * CHUNK, gp * LANES:(gp + 1) * LANES]
            s0 = _dot(ws_ref[2 * gp], vp)
            s1 = _dot(ws_ref[2 * gp + 1], vp)
            cols.append(jnp.where(lane < group_dim, s0, s1))
        rows.append(jnp.concatenate(cols, axis=-1) + bs_ref[...])
    gm = (gmu_ref[0].astype(F32) * jnp.concatenate(rows, axis=0)).astype(BF16)
    br_gmlp = _dot(gm, wgo_ref[...])

    g = gate_ref[0]
    mix = (g[:, :d].astype(F32) * br_conv + g[:, d:2 * d].astype(F32) * br_attn
           + g[:, 2 * d:].astype(F32) * br_gmlp)
    y = _dot(mix.astype(BF16), wout_ref[...])
    h_new = h_ref[0] + mod[2:3] * _rms(y, gpost_ref[...])
    hout_ref[0] = h_new
    f_in = _rms(h_new, gpre_ref[...]) * (1.0 + mod[4:5]) + mod[3:4]
    if not moe_next:
        fin_ref[0] = f_in.astype(fin_ref.dtype)
    else:
        _store_row_tiles(fin_ref.at[0], f_in)
        logits = _dot_f32(f_in, wr_hi_ref[...], wr_lo_ref[...])
        ln = lax.broadcasted_iota(jnp.int32, logits.shape, 1)
        neg = -jnp.inf
        lg = jnp.where(ln < n_experts, logits, neg)
        m1 = jnp.max(lg, axis=-1, keepdims=True)
        i1 = jnp.min(jnp.where(lg == m1, ln, LANES), axis=-1, keepdims=True)
        lg2 = jnp.where(ln == i1, neg, lg)
        m2 = jnp.max(lg2, axis=-1, keepdims=True)
        i2 = jnp.min(jnp.where(lg2 == m2, ln, LANES), axis=-1, keepdims=True)
        e2 = jnp.exp(m2 - m1)
        w1 = 1.0 / (1.0 + e2)
        w2 = e2 / (1.0 + e2)
        route = jnp.where(ln == 0, i1.astype(F32),
                          jnp.where(ln == 1, i2.astype(F32),
                                    jnp.where(ln == 2, w1, jnp.where(ln == 3, w2, 0.0))))
        route_ref[0] = route


def _mixer(h, modtab, u, o_lat, o_ctx, gmu, gmv, gates, wts, g_post, g_pre_f, n_lat_tiles, rows_out,
           router):
    bsz, nt, d = h.shape
    tm = TOKEN_TILE
    n_tiles = nt // tm
    has_ctx = o_ctx is not None
    moe_next = router is not None
    if not has_ctx:
        o_ctx = o_lat
    n_ctx_blocks = o_ctx.shape[1] // tm
    hb = tm // HALO
    last_halo = nt // HALO - 1
    tok = lambda w: pl.BlockSpec((1, tm, w), lambda b, t: (b, t, 0))
    cd = u.shape[2]
    weights = [wts[k] for k in ("conv_dw", "conv_db", "conv_ln_g", "conv_ln_b", "w_conv_out", "w_attn_out",
                                "w_spatial", "b_spatial", "w_gmlp_out", "w_out")] + [g_post, g_pre_f]
    in_specs = [
        tok(d),
        pl.BlockSpec((1, 1, N_MOD, d), lambda b, t: (b, jnp.where(t >= n_lat_tiles, 1, 0), 0, 0)),
        tok(cd),
        pl.BlockSpec((1, HALO, cd), lambda b, t: (b, jnp.maximum(t * hb - 1, 0), 0)),
        pl.BlockSpec((1, HALO, cd), lambda b, t: (b, jnp.minimum((t + 1) * hb, last_halo), 0)),
        pl.BlockSpec((1, tm, o_lat.shape[2]), lambda b, t: (b, jnp.minimum(t, n_lat_tiles - 1), 0)),
        pl.BlockSpec((1, tm, o_ctx.shape[2]),
                     lambda b, t: (b, jnp.clip(t - n_lat_tiles, 0, n_ctx_blocks - 1), 0)),
        tok(gmu.shape[2]), tok(gmv.shape[2]), tok(gates.shape[2]),
    ] + [_resident(w.shape) for w in weights]
    operands = [h, modtab, u, u, u, o_lat, o_ctx, gmu, gmv, gates] + weights
    out_specs = [tok(d), tok(d)]
    out_shape = [jax.ShapeDtypeStruct((bsz, rows_out, d), F32), jax.ShapeDtypeStruct((bsz, rows_out, d), BF16)]
    n_experts = 0
    if moe_next:
        assert d == F32_SUBLANES * LANES
        out_specs[1] = pl.BlockSpec((1, tm * F32_SUBLANES, LANES), lambda b, t: (b, t, 0))
        out_shape[1] = jax.ShapeDtypeStruct((bsz, rows_out * F32_SUBLANES, LANES), F32)
        n_experts = router[2]
        in_specs += [_resident(router[0].shape), _resident(router[1].shape)]
        operands += [router[0], router[1]]
        out_specs.append(tok(LANES))
        out_shape.append(jax.ShapeDtypeStruct((bsz, rows_out, LANES), F32))
    return pl.pallas_call(
        functools.partial(_mixer_kernel, n_lat_tiles=n_lat_tiles, n_tiles=n_tiles, has_ctx=has_ctx,
                          moe_next=moe_next, n_experts=n_experts),
        grid=(bsz, rows_out // tm),
        in_specs=in_specs,
        out_specs=out_specs,
        out_shape=out_shape,
        scratch_shapes=[pltpu.VMEM((tm + 2 * HALO, cd), F32)],
        compiler_params=_params("parallel", "parallel"),
        name="mixer",
    )(*operands)


def _swiglu_block(x, w13_ref, w2_ref, ff_tile):
    d_ff = w2_ref.shape[1]
    acc = jnp.zeros((x.shape[0], w2_ref.shape[2]), F32)
    for j in range(d_ff // ff_tile):
        a = _dot(x, w13_ref[0, :, j * ff_tile:(j + 1) * ff_tile])
        g = _dot(x, w13_ref[0, :, d_ff + j * ff_tile:d_ff + (j + 1) * ff_tile])
        acc = acc + _dot((_silu(a) * g).astype(BF16), w2_ref[0, j * ff_tile:(j + 1) * ff_tile, :])
    return acc


def _dense_ffn_kernel(x_ref, h_ref, mod_ref, gpost_ref, w13_ref, w2_ref, o_ref, *, ff_tile):
    f = _swiglu_block(x_ref[0], w13_ref, w2_ref, ff_tile)
    o_ref[0] = h_ref[0] + mod_ref[0, 0][5:6] * _rms(f, gpost_ref[...])


def _ff_tile(d_ff, target):
    n = d_ff // LANES
    best = 1
    for c in range(1, n + 1):
        if n % c == 0 and c * LANES <= target:
            best = c
    return best * LANES


def _dense_ffn(f_in, h, modtab, g_post, w13, w2, n_lat_tiles):
    bsz, rows, d = h.shape
    tm = TOKEN_TILE
    tok = lambda w: pl.BlockSpec((1, tm, w), lambda b, t: (b, t, 0))
    return pl.pallas_call(
        functools.partial(_dense_ffn_kernel, ff_tile=_ff_tile(w2.shape[1], 1536)),
        grid=(bsz, rows // tm),
        in_specs=[
            tok(d), tok(d),
            pl.BlockSpec((1, 1, N_MOD, d), lambda b, t: (b, jnp.where(t >= n_lat_tiles, 1, 0), 0, 0)),
            _resident(g_post.shape), _resident(w13.shape), _resident(w2.shape),
        ],
        out_specs=tok(d),
        out_shape=jax.ShapeDtypeStruct((bsz, rows, d), F32),
        compiler_params=_params("parallel", "parallel"),
        name="dense_ffn",
    )(f_in, h, modtab, g_post, w13, w2)


def _expert_ffn_kernel(be_ref, src_ref, src_next_ref, dst_ref, x_hbm, w13_ref, w2_ref, y_hbm,
                       xbuf, ybuf, gsem, ssem, *, ff_tile, n_blocks):
    del be_ref
    i = pl.program_id(0)
    rows = src_ref.shape[2]
    slot = i % 2
    sub = F32_SUBLANES

    def gather(idx_ref, buf_slot, unroll):
        def issue(r, carry):
            pltpu.make_async_copy(x_hbm.at[pl.ds(idx_ref[0, 0, r] * sub, sub)],
                                  xbuf.at[buf_slot, pl.ds(r * sub, sub)], gsem.at[buf_slot]).start()
            return carry
        lax.fori_loop(0, rows, issue, 0, unroll=unroll)

    def wait_rows(hbm, buf, sem, buf_slot):
        pltpu.make_async_copy(hbm.at[pl.ds(0, rows * sub)], buf.at[buf_slot], sem.at[buf_slot]).wait()

    @pl.when(i == 0)
    def _():
        gather(src_ref, 0, False)

    wait_rows(x_hbm, xbuf, gsem, slot)
    gather(src_next_ref, 1 - slot, True)
    x = _load_row_tiles(xbuf.at[slot], rows).astype(BF16)
    y = _swiglu_block(x, w13_ref, w2_ref, ff_tile)

    @pl.when(i >= 2)
    def _():
        wait_rows(y_hbm, ybuf, ssem, slot)

    _store_row_tiles(ybuf.at[slot], y)

    def scatter(r, carry):
        pltpu.make_async_copy(ybuf.at[slot, pl.ds(r * sub, sub)],
                              y_hbm.at[pl.ds(dst_ref[0, 0, r] * sub, sub)], ssem.at[slot]).start()
        return carry
    lax.fori_loop(0, rows, scatter, 0)

    @pl.when(i == n_blocks - 1)
    def _():
        wait_rows(x_hbm, xbuf, gsem, 1 - slot)
        wait_rows(y_hbm, ybuf, ssem, slot)
        if n_blocks > 1:
            wait_rows(y_hbm, ybuf, ssem, 1 - slot)


def _expert_ffn(x_tiles, block_expert, src, dst, w13, w2):
    n_rows = src.shape[0]
    rb = MOE_ROWS
    n_blocks = n_rows // rb
    _, d, two_f = w13.shape
    d_ff = w2.shape[1]
    sub = F32_SUBLANES
    idx_spec = lambda f: pl.BlockSpec((1, 1, rb), f, memory_space=pltpu.SMEM)
    grid_spec = pltpu.PrefetchScalarGridSpec(
        num_scalar_prefetch=1,
        grid=(n_blocks,),
        in_specs=[
            idx_spec(lambda i, be: (i, 0, 0)),
            idx_spec(lambda i, be: (jnp.minimum(i + 1, n_blocks - 1), 0, 0)),
            idx_spec(lambda i, be: (i, 0, 0)),
            pl.BlockSpec(memory_space=pl.ANY),
            pl.BlockSpec((1, d, two_f), lambda i, be: (be[i], 0, 0), pipeline_mode=pl.Buffered(1)),
            pl.BlockSpec((1, d_ff, d), lambda i, be: (be[i], 0, 0), pipeline_mode=pl.Buffered(1)),
        ],
        out_specs=pl.BlockSpec(memory_space=pl.ANY),
        scratch_shapes=[
            pltpu.VMEM((2, rb * sub, LANES), F32),
            pltpu.VMEM((2, rb * sub, LANES), F32),
            pltpu.SemaphoreType.DMA((2,)),
            pltpu.SemaphoreType.DMA((2,)),
        ],
    )
    src3 = src.reshape(n_blocks, 1, rb)
    return pl.pallas_call(
        functools.partial(_expert_ffn_kernel, ff_tile=_ff_tile(d_ff, 512), n_blocks=n_blocks),
        grid_spec=grid_spec,
        out_shape=jax.ShapeDtypeStruct((n_rows * sub, LANES), F32),
        compiler_params=_params("arbitrary"),
        name="expert_ffn",
    )(block_expert, src3, src3, dst.reshape(n_blocks, 1, rb), x_tiles, w13, w2)


def _combine_kernel(y0_ref, y1_ref, route_ref, h_ref, mod_ref, gpost_ref, o_ref):
    route = route_ref[0]
    tm = route.shape[0]
    f = route[:, 2:3] * _load_row_tiles(y0_ref, tm) + route[:, 3:4] * _load_row_tiles(y1_ref, tm)
    o_ref[0] = h_ref[0] + mod_ref[0, 0][5:6] * _rms(f, gpost_ref[...])


def _combine(y_tiles, route, h, modtab, g_post, n_lat_tiles):
    bsz, rows, d = h.shape
    tm = TOKEN_TILE
    tpb = rows // tm
    second = bsz * tpb
    tok = lambda w: pl.BlockSpec((1, tm, w), lambda b, t: (b, t, 0))
    return pl.pallas_call(
        _combine_kernel,
        grid=(bsz, tpb),
        in_specs=[
            pl.BlockSpec((tm * F32_SUBLANES, LANES), lambda b, t: (b * tpb + t, 0)),
            pl.BlockSpec((tm * F32_SUBLANES, LANES), lambda b, t: (second + b * tpb + t, 0)),
            tok(LANES), tok(d),
            pl.BlockSpec((1, 1, N_MOD, d), lambda b, t: (b, jnp.where(t >= n_lat_tiles, 1, 0), 0, 0)),
            _resident(g_post.shape),
        ],
        out_specs=tok(d),
        out_shape=jax.ShapeDtypeStruct((bsz, rows, d), F32),
        compiler_params=_params("parallel", "parallel"),
        name="moe_combine",
    )(y_tiles, y_tiles, route, h, modtab, g_post)


def _moe_ffn(f_tiles, route, h, modtab, g_post, w13, w2, n_lat_tiles):
    bsz, rows, d = h.shape
    n_tok = bsz * rows
    n_exp = w13.shape[0]
    n_assign = n_tok * TOP_K
    rb = MOE_ROWS
    expert = route[..., :TOP_K].astype(jnp.int32).reshape(n_assign)
    onehot = (expert[:, None] == jnp.arange(n_exp, dtype=jnp.int32)[None, :]).astype(jnp.int32)
    csum = jnp.cumsum(onehot, axis=0)
    rank = jnp.sum(csum * onehot, axis=1) - 1
    counts = csum[-1]
    padded = (counts + rb - 1) // rb * rb
    padded_end = jnp.cumsum(padded)
    dest = (padded_end - padded)[expert] + rank
    n_rows = n_assign + n_exp * rb
    held = jnp.full((n_rows,), -1, jnp.int32).at[dest].set(jnp.arange(n_assign, dtype=jnp.int32))
    is_pad = held < 0
    src = jnp.where(is_pad, 0, held // TOP_K)
    spare = n_assign + jnp.cumsum(is_pad.astype(jnp.int32)) - 1
    dst = jnp.where(is_pad, spare, (held % TOP_K) * n_tok + held // TOP_K)
    block_start = jnp.arange(n_rows // rb, dtype=jnp.int32) * rb
    block_expert = jnp.minimum(
        jnp.sum((padded_end[None, :] <= block_start[:, None]).astype(jnp.int32), axis=1), n_exp - 1)

    x_tiles = f_tiles.reshape(n_tok * F32_SUBLANES, LANES)
    y_tiles = _expert_ffn(x_tiles, block_expert, src, dst, w13, w2)
    return _combine(y_tiles, route, h, modtab, g_post, n_lat_tiles)


def _rot_cols(r):
    q = QK_ROPE // 4
    return jnp.concatenate([-r[..., q:2 * q], r[..., :q], -r[..., 3 * q:], r[..., 2 * q:3 * q]], axis=-1)


def _head_pad(nope, rope):
    pad = jnp.zeros(nope.shape[:-1] + (HEAD_PAD - QK_NOPE - QK_ROPE,), nope.dtype)
    out = jnp.concatenate([nope, rope, pad], axis=-1)
    return out.reshape(out.shape[:-2] + (out.shape[-2] * HEAD_PAD,))


def _value_pad(wv):
    k, n_heads, _ = wv.shape
    tiles = []
    for hd in range(n_heads):
        pos = hd % HEADS_PER_STEP
        tiles.append(jnp.pad(wv[:, hd], ((0, 0), (pos * V_HEAD, HEAD_PAD - (pos + 1) * V_HEAD))))
    return jnp.concatenate(tiles, axis=1)


def _layer_weights(layer, w_in, b_gate, conv_dw, conv_db, conv_ln_g, conv_ln_b, w_conv_out, q_norm_g, w_uq,
                   kv_norm_g, w_ukv, w_attn_out, gmlp_ln_g, gmlp_ln_b, w_spatial, b_spatial, w_gmlp_out,
                   w_out):
    d = w_in.shape[1]
    conv_dim = conv_dw.shape[2]
    q_lora, kv_lora = w_uq.shape[1], w_ukv.shape[1]
    gmlp_dim = w_gmlp_out.shape[1]
    col_q = 2 * conv_dim
    col_kv = col_q + q_lora
    col_kr = col_kv + kv_lora
    col_gm = col_kr + QK_ROPE
    col_gate = col_gm + 2 * gmlp_dim
    wi = w_in[layer]
    row = lambda v: v[layer].reshape(1, -1)

    uq = w_uq[layer].reshape(q_lora, MLA_HEADS, QK_NOPE + QK_ROPE)
    uq_nope, uq_rope = uq[..., :QK_NOPE], uq[..., QK_NOPE:]
    ukv = w_ukv[layer].reshape(kv_lora, MLA_HEADS, QK_NOPE + V_HEAD)
    wkr = wi[:, col_kr:col_gm]
    zeros_nope = jnp.zeros((d, 1, QK_NOPE), F32)
    groups = w_spatial.shape[1]
    return {
        "w_conv": wi[:, :col_q].astype(BF16),
        "w_cq": wi[:, col_q:col_kv].astype(BF16),
        "w_ckv": wi[:, col_kv:col_kr].astype(BF16),
        "w_kr": jnp.concatenate([_head_pad(zeros_nope, wkr[:, None, :]),
                                 _head_pad(zeros_nope, _rot_cols(wkr)[:, None, :])], axis=1).astype(BF16),
        "w_gm": wi[:, col_gm:col_gate].astype(BF16),
        "w_gate": wi[:, col_gate:].astype(BF16),
        "b_gate": row(b_gate),
        "q_norm_g": row(q_norm_g),
        "w_uq": _head_pad(uq_nope, uq_rope).astype(BF16),
        "w_uq_rot": _head_pad(jnp.zeros_like(uq_nope), _rot_cols(uq_rope)).astype(BF16),
        "kv_norm_g": row(kv_norm_g),
        "w_k": _head_pad(ukv[..., :QK_NOPE], jnp.zeros((kv_lora, MLA_HEADS, QK_ROPE), F32)).astype(BF16),
        "w_v": _value_pad(ukv[..., QK_NOPE:]).astype(BF16),
        "v_one": jnp.zeros((MLA_HEADS, HEAD_PAD), F32).at[
            jnp.arange(MLA_HEADS), jnp.array([_one_lane(hd) for hd in range(MLA_HEADS)])
        ].set(1.0).reshape(1, MLA_HEADS * HEAD_PAD),
        "gmlp_ln_g": row(gmlp_ln_g),
        "gmlp_ln_b": row(gmlp_ln_b),
        "conv_dw": conv_dw[layer],
        "conv_db": row(conv_db),
        "conv_ln_g": row(conv_ln_g),
        "conv_ln_b": row(conv_ln_b),
        "w_conv_out": w_conv_out[layer].astype(BF16),
        "w_attn_out": w_attn_out[layer].astype(BF16),
        "w_spatial": w_spatial[layer].astype(BF16),
        "b_spatial": jnp.repeat(b_spatial[layer].T, gmlp_dim // groups, axis=1),
        "w_gmlp_out": w_gmlp_out[layer].astype(BF16),
        "w_out": w_out[layer].astype(BF16),
    }


def _rope_tables(n_lat, n_ctx):
    pairs = QK_ROPE // 4
    pos = jnp.arange(n_lat, dtype=jnp.int32)
    row = (pos // GRID_W).astype(F32)
    col = (pos % GRID_W).astype(F32)
    inv_freq = ROPE_THETA ** (-jnp.arange(pairs, dtype=F32) / pairs)
    ar, ac = row[:, None] * inv_freq, col[:, None] * inv_freq

    def table(fn, fill):
        rope = jnp.concatenate([fn(ar), fn(ar), fn(ac), fn(ac)], axis=-1)
        lat = jnp.concatenate([jnp.full((n_lat, QK_NOPE), fill, F32), rope,
                               jnp.full((n_lat, HEAD_PAD - QK_NOPE - QK_ROPE), fill, F32)], axis=-1)
        return jnp.concatenate([lat, jnp.full((n_ctx, HEAD_PAD), fill, F32)], axis=0)

    return table(jnp.cos, 1.0), table(jnp.sin, 0.0)


def kernel(x, c, ctx, c_ctx, w_mod, b_mod, norm_g, w_in, b_gate, conv_dw, conv_db, conv_ln_g, conv_ln_b,
           w_conv_out, q_norm_g, w_uq, kv_norm_g, w_ukv, w_attn_out, gmlp_ln_g, gmlp_ln_b, w_spatial,
           b_spatial, w_gmlp_out, w_out, ffn_w13, ffn_w2, router_w, moe_w13, moe_w2):
    bsz, n_lat, d = x.shape
    n_ctx = ctx.shape[1]
    depth = w_mod.shape[0]
    tm = TOKEN_TILE
    assert n_lat % Q_TILE == 0 and n_lat % n_ctx == 0 and n_ctx % tm == 0 and n_lat % GRID_W == 0
    assert bsz + 1 <= MOD_ROWS and w_spatial.shape[2] == CHUNK and V_HEAD * HEADS_PER_STEP == LANES
    n_lat_tiles = n_lat // tm

    c_rows = jnp.zeros((MOD_ROWS, d), F32).at[:bsz].set(c).at[bsz].set(c_ctx)
    mod_all = _modulation(c_rows, w_mod, b_mod)
    cos_t, sin_t = _rope_tables(n_lat, n_ctx)
    h = jnp.concatenate([x, ctx], axis=1)

    for layer in range(depth):
        need_ctx = layer < depth - 1
        rows_out = n_lat + n_ctx if need_ctx else n_lat
        mod = mod_all[layer].reshape(MOD_ROWS, N_MOD, d)
        modtab = jnp.stack([mod[:bsz], jnp.broadcast_to(mod[bsz], (bsz, N_MOD, d))], axis=1)
        g = norm_g[layer].reshape(4, 1, d)
        wts = _layer_weights(layer, w_in, b_gate, conv_dw, conv_db, conv_ln_g, conv_ln_b, w_conv_out,
                             q_norm_g, w_uq, kv_norm_g, w_ukv, w_attn_out, gmlp_ln_g, gmlp_ln_b, w_spatial,
                             b_spatial, w_gmlp_out, w_out)

        u, q, k, v, gmu, gmv, gates = _inproj(h, modtab, g[0], cos_t, sin_t, wts, n_lat_tiles)
        o_lat = _attention(q, k, v, 0, n_lat, 0, n_lat + n_ctx)
        o_ctx = _attention(q, k, v, n_lat, n_ctx, n_lat, n_ctx) if need_ctx else None

        idx = layer // 2
        router = None
        if layer % 2 == 1:
            n_exp = router_w.shape[2]
            wr = jnp.zeros((d, LANES), F32).at[:, :n_exp].set(router_w[idx])
            wr_hi = wr.astype(BF16)
            router = (wr_hi, (wr - wr_hi.astype(F32)).astype(BF16), n_exp)
        outs = _mixer(h, modtab, u, o_lat, o_ctx, gmu, gmv, gates, wts, g[1], g[2], n_lat_tiles, rows_out,
                      router)
        if router is None:
            h_mid, f_in = outs
            h = _dense_ffn(f_in, h_mid, modtab, g[3], ffn_w13[idx].astype(BF16)[None],
                           ffn_w2[idx].astype(BF16)[None], n_lat_tiles)
        else:
            h_mid, f_in, route = outs
            h = _moe_ffn(f_in, route, h_mid, modtab, g[3], moe_w13[idx].astype(BF16),
                         moe_w2[idx].astype(BF16), n_lat_tiles)
    return h[:, :n_lat]
```

```python
import functools
import math

import jax
import jax.numpy as jnp
from jax import lax
from jax.experimental import pallas as pl
from jax.experimental.pallas import tpu as pltpu

F32 = jnp.float32
BF16 = jnp.bfloat16

GRID_W = 64
MLA_HEADS = 8
QK_NOPE = 64
QK_ROPE = 32
V_HEAD = 64
ROPE_THETA = 10000.0
CHUNK = 128
TOP_K = 2
N_MOD = 6
RMS_EPS = 1e-6
LN_EPS = 1e-5

LANES = 128
F32_SUBLANES = 8
BF16_SUBLANES = 16
VMEM_LIMIT_BYTES = 56 * 1024 * 1024

HEAD_PAD = LANES
HEADS_PER_STEP = LANES // V_HEAD
TOKEN_TILE = 256
HALO = BF16_SUBLANES
CONV_ROWS = 128
Q_TILE = 512
KV_TILE = 256
MOE_ROWS = 512
MOD_ROWS = 16


def _params(*sem):
    return pltpu.CompilerParams(dimension_semantics=sem, vmem_limit_bytes=VMEM_LIMIT_BYTES)


def _resident(shape):
    zeros = (0,) * len(shape)
    return pl.BlockSpec(shape, lambda *_: zeros, pipeline_mode=pl.Buffered(1))


def _dot(a, b):
    return jnp.dot(a, b, preferred_element_type=F32)


def _split_bf16(x):
    hi = x.astype(BF16)
    lo = (x - hi.astype(F32)).astype(BF16)
    return hi, lo


def _dot_f32(x, w_hi, w_lo):
    x_hi, x_lo = _split_bf16(x)
    return _dot(x_hi, w_hi) + (_dot(x_lo, w_hi) + _dot(x_hi, w_lo))


def _rms(x, g):
    return x * lax.rsqrt(jnp.mean(x * x, axis=-1, keepdims=True) + RMS_EPS) * g


def _layer_norm(x, g, b):
    xc = x - jnp.mean(x, axis=-1, keepdims=True)
    var = jnp.mean(xc * xc, axis=-1, keepdims=True)
    return xc * lax.rsqrt(var + LN_EPS) * g + b


def _silu(x):
    return x * jax.nn.sigmoid(x)


def _load_row_tiles(ref, rows):
    return jnp.concatenate([ref[pl.ds(s, rows, stride=F32_SUBLANES), :] for s in range(F32_SUBLANES)], axis=1)


def _store_row_tiles(ref, x):
    for s in range(F32_SUBLANES):
        ref[pl.ds(s, x.shape[0], stride=F32_SUBLANES), :] = x[:, s * LANES:(s + 1) * LANES]


def _mod_kernel(c_ref, w_ref, b_ref, o_ref):
    w_hi, w_lo = _split_bf16(w_ref[0])
    o_ref[0] = _dot_f32(_silu(c_ref[...]), w_hi, w_lo) + b_ref[0]


def _modulation(c_rows, w_mod, b_mod):
    n_layers, d, n_out = w_mod.shape
    return pl.pallas_call(
        _mod_kernel,
        grid=(n_layers, n_out // d),
        in_specs=[
            pl.BlockSpec((MOD_ROWS, d), lambda l, j: (0, 0)),
            pl.BlockSpec((1, d, d), lambda l, j: (l, 0, j)),
            pl.BlockSpec((1, 1, d), lambda l, j: (l, 0, j)),
        ],
        out_specs=pl.BlockSpec((1, MOD_ROWS, d), lambda l, j: (l, 0, j)),
        out_shape=jax.ShapeDtypeStruct((n_layers, MOD_ROWS, n_out), F32),
        compiler_params=_params("arbitrary", "arbitrary"),
        name="modulation",
    )(c_rows, w_mod, b_mod.reshape(n_layers, 1, n_out))


def _inproj_kernel(h_ref, mod_ref, gpre_ref, cos_ref, sin_ref,
                   w_conv_ref, w_cq_ref, w_ckv_ref, w_kr_ref, w_gm_ref, w_gate_ref, b_gate_ref,
                   qg_ref, wuq_ref, wuq_rot_ref, kvg_ref, wk_ref, wv_ref, vone_ref, gln_g_ref, gln_b_ref,
                   u_ref, q_ref, k_ref, v_ref, gmu_ref, gmv_ref, gate_ref, *, scale):
    mod = mod_ref[0, 0]
    a = (_rms(h_ref[0], gpre_ref[...]) * (1.0 + mod[1:2]) + mod[0:1]).astype(BF16)
    cos = cos_ref[...]
    sin = sin_ref[...]

    zc = _dot(a, w_conv_ref[...])
    half = zc.shape[1] // 2
    u_ref[0] = (zc[:, :half] * jax.nn.sigmoid(zc[:, half:])).astype(BF16)

    cq = _rms(_dot(a, w_cq_ref[...]), qg_ref[...]).astype(BF16)
    qf = _dot(cq, wuq_ref[...])
    qr = _dot(cq, wuq_rot_ref[...])
    for hd in range(qf.shape[1] // HEAD_PAD):
        sl = slice(hd * HEAD_PAD, (hd + 1) * HEAD_PAD)
        q_ref[0, :, sl] = ((qf[:, sl] * cos + qr[:, sl] * sin) * scale).astype(BF16)

    ckv = _rms(_dot(a, w_ckv_ref[...]), kvg_ref[...]).astype(BF16)
    kr = _dot(a, w_kr_ref[...])
    kr = kr[:, :HEAD_PAD] * cos + kr[:, HEAD_PAD:] * sin
    kn = _dot(ckv, wk_ref[...])
    for hd in range(kn.shape[1] // HEAD_PAD):
        sl = slice(hd * HEAD_PAD, (hd + 1) * HEAD_PAD)
        k_ref[0, :, sl] = (kn[:, sl] + kr).astype(BF16)
    v_ref[0] = (_dot(ckv, wv_ref[...]) + vone_ref[...]).astype(BF16)

    zg = jax.nn.gelu(_dot(a, w_gm_ref[...]), approximate=True)
    gd = zg.shape[1] // 2
    gmu_ref[0] = zg[:, :gd].astype(BF16)
    gmv_ref[0] = _layer_norm(zg[:, gd:], gln_g_ref[...], gln_b_ref[...]).astype(BF16)

    gate_ref[0] = jax.nn.sigmoid(_dot(a, w_gate_ref[...]) + b_gate_ref[...]).astype(BF16)


def _inproj(h, modtab, g_pre, cos_t, sin_t, wts, n_lat_tiles):
    bsz, nt, d = h.shape
    tm = TOKEN_TILE
    tok = lambda w: pl.BlockSpec((1, tm, w), lambda b, t: (b, t, 0))
    weights = [wts[k] for k in ("w_conv", "w_cq", "w_ckv", "w_kr", "w_gm", "w_gate", "b_gate", "q_norm_g",
                                "w_uq", "w_uq_rot", "kv_norm_g", "w_k", "w_v", "v_one", "gmlp_ln_g",
                                "gmlp_ln_b")]
    widths = (wts["w_conv"].shape[1] // 2, wts["w_uq"].shape[1], wts["w_k"].shape[1], wts["w_v"].shape[1],
              wts["w_gm"].shape[1] // 2, wts["w_gm"].shape[1] // 2, wts["w_gate"].shape[1])
    scale = math.log2(math.e) / math.sqrt(QK_NOPE + QK_ROPE)
    return pl.pallas_call(
        functools.partial(_inproj_kernel, scale=scale),
        grid=(bsz, nt // tm),
        in_specs=[
            tok(d),
            pl.BlockSpec((1, 1, N_MOD, d), lambda b, t: (b, jnp.where(t >= n_lat_tiles, 1, 0), 0, 0)),
            _resident(g_pre.shape),
            pl.BlockSpec((tm, LANES), lambda b, t: (t, 0)),
            pl.BlockSpec((tm, LANES), lambda b, t: (t, 0)),
        ] + [_resident(w.shape) for w in weights],
        out_specs=[tok(w) for w in widths],
        out_shape=[jax.ShapeDtypeStruct((bsz, nt, w), BF16) for w in widths],
        compiler_params=_params("parallel", "parallel"),
        name="inproj",
    )(h, modtab, g_pre, cos_t, sin_t, *weights)


def _one_lane(head):
    return ((head % HEADS_PER_STEP + 1) % HEADS_PER_STEP) * V_HEAD


def _attn_kernel(q_ref, k_ref, v_ref, o_ref, *, kv_tile):
    tq = q_ref.shape[1]
    n_kv = k_ref.shape[1] // kv_tile
    heads = [slice(hd * HEAD_PAD, (hd + 1) * HEAD_PAD) for hd in range(HEADS_PER_STEP)]
    qs = [q_ref[0, :, hs] for hs in heads]

    def body(j, carry):
        rows = pl.ds(pl.multiple_of(j * kv_tile, kv_tile), kv_tile)
        out = []
        for hs, q, (m, acc) in zip(heads, qs, carry):
            s = lax.dot_general(q, k_ref[0, rows, hs], (((1,), (1,)), ((), ())),
                                preferred_element_type=F32)
            m_new = jnp.maximum(m, jnp.max(s, axis=-1, keepdims=True))
            p = jnp.exp2(s - m_new).astype(BF16)
            acc = jnp.exp2(m - m_new) * acc + _dot(p, v_ref[0, rows, hs])
            out.append((m_new, acc))
        return tuple(out)

    init = tuple((jnp.full((tq, 1), -1e30, F32), jnp.zeros((tq, LANES), F32)) for _ in heads)
    carry = lax.fori_loop(0, n_kv, body, init, unroll=True)
    lane = lax.broadcasted_iota(jnp.int32, (tq, LANES), 1)
    o = None
    for hd in range(HEADS_PER_STEP - 1, -1, -1):
        acc = carry[hd][1]
        one = _one_lane(hd)
        o_hd = acc / acc[:, one:one + 1]
        o = o_hd if o is None else jnp.where(lane < (hd + 1) * V_HEAD, o_hd, o)
    o_ref[0] = o.astype(BF16)


def _attention(q, k, v, q_start, q_len, kv_start, kv_len):
    bsz = q.shape[0]
    qw = HEADS_PER_STEP * HEAD_PAD
    n_pairs = v.shape[2] // qw
    tq = min(Q_TILE, q_len)
    kv_tile = KV_TILE if kv_len % KV_TILE == 0 else math.gcd(kv_len, KV_TILE)
    q_off, kv_blk = q_start // tq, kv_start // kv_len
    return pl.pallas_call(
        functools.partial(_attn_kernel, kv_tile=kv_tile),
        grid=(bsz, n_pairs, q_len // tq),
        in_specs=[
            pl.BlockSpec((1, tq, qw), lambda b, p, i: (b, q_off + i, p)),
            pl.BlockSpec((1, kv_len, qw), lambda b, p, i: (b, kv_blk, p)),
            pl.BlockSpec((1, kv_len, qw), lambda b, p, i: (b, kv_blk, p)),
        ],
        out_specs=pl.BlockSpec((1, tq, LANES), lambda b, p, i: (b, i, p)),
        out_shape=jax.ShapeDtypeStruct((bsz, q_len, n_pairs * LANES), BF16),
        compiler_params=_params("parallel", "parallel", "arbitrary"),
        name="attention",
    )(q, k, v)


def _mixer_kernel(*refs, n_lat_tiles, n_tiles, has_ctx, moe_next, n_experts):
    (h_ref, mod_ref, u_ref, up_ref, un_ref, ol_ref, oc_ref, gmu_ref, gmv_ref, gate_ref,
     dw_ref, db_ref, cln_g_ref, cln_b_ref, wco_ref, wao_ref, ws_ref, bs_ref, wgo_ref, wout_ref,
     gpost_ref, gpre_ref) = refs[:22]
    rest = refs[22:]
    if moe_next:
        wr_hi_ref, wr_lo_ref, hout_ref, fin_ref, route_ref, win_ref = rest
    else:
        hout_ref, fin_ref, win_ref = rest
    t = pl.program_id(1)
    tm = h_ref.shape[1]
    d = h_ref.shape[2]
    mod = mod_ref[0, 0]

    prev_ok = jnp.logical_and(t != 0, t != n_lat_tiles)
    next_ok = jnp.logical_and(t != n_lat_tiles - 1, t != n_tiles - 1)
    win_ref[0:HALO, :] = jnp.where(prev_ok, up_ref[0].astype(F32), 0.0)
    win_ref[HALO:HALO + tm, :] = u_ref[0].astype(F32)
    win_ref[HALO + tm:, :] = jnp.where(next_ok, un_ref[0].astype(F32), 0.0)
    n_taps = dw_ref.shape[0]
    first = HALO - n_taps // 2
    sub = F32_SUBLANES
    span = CONV_ROWS + (first + n_taps - 1) // sub * sub + sub
    chunks = []
    for c in range(dw_ref.shape[1] // LANES):
        cs = slice(c * LANES, (c + 1) * LANES)
        blocks = []
        for r0 in range(0, tm, CONV_ROWS):
            acc = jnp.zeros((CONV_ROWS, LANES), F32)
            window = win_ref[r0:r0 + span, cs]
            for phase in range(sub):
                taps = [kk for kk in range(n_taps) if (first + kk) % sub == phase]
                if taps:
                    ws = window if phase == 0 else pltpu.roll(window, span - phase, axis=0)
                    for kk in taps:
                        a = (first + kk) // sub * sub
                        acc = acc + ws[a:a + CONV_ROWS] * dw_ref[kk:kk + 1, cs]
            blocks.append(acc)
        chunks.append(jnp.concatenate(blocks, axis=0))
    conv = jnp.concatenate(chunks, axis=-1) + db_ref[...]
    c_act = _silu(_layer_norm(conv, cln_g_ref[...], cln_b_ref[...])).astype(BF16)
    br_conv = _dot(c_act, wco_ref[...])

    o = ol_ref[0]
    if has_ctx:
        o = jnp.where(t >= n_lat_tiles, oc_ref[0], o)
    br_attn = _dot(o, wao_ref[...])

    gmv = gmv_ref[0]
    lane = lax.broadcasted_iota(jnp.int32, (CHUNK, LANES), 1)
    group_dim = gmv.shape[1] // ws_ref.shape[0]
    rows = []
    for j in range(tm // CHUNK):
        cols = []
        for gp in range(gmv.shape[1] // LANES):
            vp = gmv[j * CHUNK:(j + 1) * CHUNK, gp * LANES:(gp + 1) * LANES]
            s0 = _dot(ws_ref[2 * gp], vp)
            s1 = _dot(ws_ref[2 * gp + 1], vp)
            cols.append(jnp.where(lane < group_dim, s0, s1))
        rows.append(jnp.concatenate(cols, axis=-1) + bs_ref[...])
    gm = (gmu_ref[0].astype(F32) * jnp.concatenate(rows, axis=0)).astype(BF16)
    br_gmlp = _dot(gm, wgo_ref[...])

    g = gate_ref[0]
    mix = (g[:, :d].astype(F32) * br_conv + g[:, d:2 * d].astype(F32) * br_attn
           + g[:, 2 * d:].astype(F32) * br_gmlp)
    y = _dot(mix.astype(BF16), wout_ref[...])
    h_new = h_ref[0] + mod[2:3] * _rms(y, gpost_ref[...])
    hout_ref[0] = h_new
    f_in = _rms(h_new, gpre_ref[...]) * (1.0 + mod[4:5]) + mod[3:4]
    if not moe_next:
        fin_ref[0] = f_in.astype(fin_ref.dtype)
    else:
        _store_row_tiles(fin_ref.at[0], f_in)
        logits = _dot_f32(f_in, wr_hi_ref[...], wr_lo_ref[...])
        ln = lax.broadcasted_iota(jnp.int32, logits.shape, 1)
        neg = -jnp.inf
        lg = jnp.where(ln < n_experts, logits, neg)
        m1 = jnp.max(lg, axis=-1, keepdims=True)
        i1 = jnp.min(jnp.where(lg == m1, ln, LANES), axis=-1, keepdims=True)
        lg2 = jnp.where(ln == i1, neg, lg)
        m2 = jnp.max(lg2, axis=-1, keepdims=True)
        i2 = jnp.min(jnp.where(lg2 == m2, ln, LANES), axis=-1, keepdims=True)
        e2 = jnp.exp(m2 - m1)
        w1 = 1.0 / (1.0 + e2)
        w2 = e2 / (1.0 + e2)
        route = jnp.where(ln == 0, i1.astype(F32),
                          jnp.where(ln == 1, i2.astype(F32),
                                    jnp.where(ln == 2, w1, jnp.where(ln == 3, w2, 0.0))))
        route_ref[0] = route


def _mixer(h, modtab, u, o_lat, o_ctx, gmu, gmv, gates, wts, g_post, g_pre_f, n_lat_tiles, rows_out,
           router):
    bsz, nt, d = h.shape
    tm = TOKEN_TILE
    n_tiles = nt // tm
    has_ctx = o_ctx is not None
    moe_next = router is not None
    if not has_ctx:
        o_ctx = o_lat
    n_ctx_blocks = o_ctx.shape[1] // tm
    hb = tm // HALO
    last_halo = nt // HALO - 1
    tok = lambda w: pl.BlockSpec((1, tm, w), lambda b, t: (b, t, 0))
    cd = u.shape[2]
    weights = [wts[k] for k in ("conv_dw", "conv_db", "conv_ln_g", "conv_ln_b", "w_conv_out", "w_attn_out",
                                "w_spatial", "b_spatial", "w_gmlp_out", "w_out")] + [g_post, g_pre_f]
    in_specs = [
        tok(d),
        pl.BlockSpec((1, 1, N_MOD, d), lambda b, t: (b, jnp.where(t >= n_lat_tiles, 1, 0), 0, 0)),
        tok(cd),
        pl.BlockSpec((1, HALO, cd), lambda b, t: (b, jnp.maximum(t * hb - 1, 0), 0)),
        pl.BlockSpec((1, HALO, cd), lambda b, t: (b, jnp.minimum((t + 1) * hb, last_halo), 0)),
        pl.BlockSpec((1, tm, o_lat.shape[2]), lambda b, t: (b, jnp.minimum(t, n_lat_tiles - 1), 0)),
        pl.BlockSpec((1, tm, o_ctx.shape[2]),
                     lambda b, t: (b, jnp.clip(t - n_lat_tiles, 0, n_ctx_blocks - 1), 0)),
        tok(gmu.shape[2]), tok(gmv.shape[2]), tok(gates.shape[2]),
    ] + [_resident(w.shape) for w in weights]
    operands = [h, modtab, u, u, u, o_lat, o_ctx, gmu, gmv, gates] + weights
    out_specs = [tok(d), tok(d)]
    out_shape = [jax.ShapeDtypeStruct((bsz, rows_out, d), F32), jax.ShapeDtypeStruct((bsz, rows_out, d), BF16)]
    n_experts = 0
    if moe_next:
        assert d == F32_SUBLANES * LANES
        out_specs[1] = pl.BlockSpec((1, tm * F32_SUBLANES, LANES), lambda b, t: (b, t, 0))
        out_shape[1] = jax.ShapeDtypeStruct((bsz, rows_out * F32_SUBLANES, LANES), F32)
        n_experts = router[2]
        in_specs += [_resident(router[0].shape), _resident(router[1].shape)]
        operands += [router[0], router[1]]
        out_specs.append(tok(LANES))
        out_shape.append(jax.ShapeDtypeStruct((bsz, rows_out, LANES), F32))
    return pl.pallas_call(
        functools.partial(_mixer_kernel, n_lat_tiles=n_lat_tiles, n_tiles=n_tiles, has_ctx=has_ctx,
                          moe_next=moe_next, n_experts=n_experts),
        grid=(bsz, rows_out // tm),
        in_specs=in_specs,
        out_specs=out_specs,
        out_shape=out_shape,
        scratch_shapes=[pltpu.VMEM((tm + 2 * HALO, cd), F32)],
        compiler_params=_params("parallel", "parallel"),
        name="mixer",
    )(*operands)


def _swiglu_block(x, w13_ref, w2_ref, ff_tile, after_matmul=None):
    d_ff = w2_ref.shape[1]
    n_chunks = d_ff // ff_tile
    done = [0]

    def dot_then_side_work(a, b):
        out = _dot(a, b)
        if after_matmul is not None:
            after_matmul(done[0], 3 * n_chunks)
        done[0] += 1
        return out

    acc = jnp.zeros((x.shape[0], w2_ref.shape[2]), F32)
    for j in range(n_chunks):
        a = dot_then_side_work(x, w13_ref[0, :, j * ff_tile:(j + 1) * ff_tile])
        g = dot_then_side_work(x, w13_ref[0, :, d_ff + j * ff_tile:d_ff + (j + 1) * ff_tile])
        acc = acc + dot_then_side_work((_silu(a) * g).astype(BF16), w2_ref[0, j * ff_tile:(j + 1) * ff_tile, :])
    return acc


def _dense_ffn_kernel(x_ref, h_ref, mod_ref, gpost_ref, w13_ref, w2_ref, o_ref, *, ff_tile):
    f = _swiglu_block(x_ref[0], w13_ref, w2_ref, ff_tile)
    o_ref[0] = h_ref[0] + mod_ref[0, 0][5:6] * _rms(f, gpost_ref[...])


def _ff_tile(d_ff, target):
    n = d_ff // LANES
    best = 1
    for c in range(1, n + 1):
        if n % c == 0 and c * LANES <= target:
            best = c
    return best * LANES


def _dense_ffn(f_in, h, modtab, g_post, w13, w2, n_lat_tiles):
    bsz, rows, d = h.shape
    tm = TOKEN_TILE
    tok = lambda w: pl.BlockSpec((1, tm, w), lambda b, t: (b, t, 0))
    return pl.pallas_call(
        functools.partial(_dense_ffn_kernel, ff_tile=_ff_tile(w2.shape[1], 1536)),
        grid=(bsz, rows // tm),
        in_specs=[
            tok(d), tok(d),
            pl.BlockSpec((1, 1, N_MOD, d), lambda b, t: (b, jnp.where(t >= n_lat_tiles, 1, 0), 0, 0)),
            _resident(g_post.shape), _resident(w13.shape), _resident(w2.shape),
        ],
        out_specs=tok(d),
        out_shape=jax.ShapeDtypeStruct((bsz, rows, d), F32),
        compiler_params=_params("parallel", "parallel"),
        name="dense_ffn",
    )(f_in, h, modtab, g_post, w13, w2)


def _expert_ffn_kernel(be_ref, src_ref, src_next_ref, dst_prev_ref, dst_ref, x_hbm, w13_ref, w2_ref, y_hbm,
                       xbuf, ybuf, gsem, ssem, *, ff_tile, n_blocks):
    del be_ref
    i = pl.program_id(0)
    rows = src_ref.shape[2]
    slot = i % 2
    other = 1 - slot
    sub = F32_SUBLANES

    def gather_row(idx_ref, buf_slot, r):
        pltpu.make_async_copy(x_hbm.at[pl.ds(idx_ref[0, 0, r] * sub, sub)],
                              xbuf.at[buf_slot, pl.ds(r * sub, sub)], gsem.at[buf_slot]).start()

    def scatter_row(idx_ref, buf_slot, r):
        pltpu.make_async_copy(ybuf.at[buf_slot, pl.ds(r * sub, sub)],
                              y_hbm.at[pl.ds(idx_ref[0, 0, r] * sub, sub)], ssem.at[buf_slot]).start()

    def all_rows(row_fn):
        def body(r, carry):
            row_fn(r)
            return carry
        lax.fori_loop(0, rows, body, 0)

    def wait_rows(hbm, buf, sem, buf_slot):
        pltpu.make_async_copy(hbm.at[pl.ds(0, rows * sub)], buf.at[buf_slot], sem.at[buf_slot]).wait()

    @pl.when(i == 0)
    def _():
        all_rows(functools.partial(gather_row, src_ref, 0))
        ybuf[1] = jnp.zeros(ybuf.shape[1:], F32)

    wait_rows(x_hbm, xbuf, gsem, slot)

    def side_copies(k, n):
        per = -(-rows // n)
        for r in range(k * per, min((k + 1) * per, rows)):
            gather_row(src_next_ref, other, r)
            scatter_row(dst_prev_ref, other, r)

    x = _load_row_tiles(xbuf.at[slot], rows).astype(BF16)
    y = _swiglu_block(x, w13_ref, w2_ref, ff_tile, side_copies)

    @pl.when(i >= 1)
    def _():
        wait_rows(y_hbm, ybuf, ssem, slot)

    _store_row_tiles(ybuf.at[slot], y)

    @pl.when(i == n_blocks - 1)
    def _():
        all_rows(functools.partial(scatter_row, dst_ref, slot))
        wait_rows(x_hbm, xbuf, gsem, other)
        wait_rows(y_hbm, ybuf, ssem, other)
        wait_rows(y_hbm, ybuf, ssem, slot)


def _expert_ffn(x_tiles, block_expert, src, dst, w13, w2):
    n_rows = src.shape[0]
    rb = MOE_ROWS
    n_blocks = n_rows // rb
    _, d, two_f = w13.shape
    d_ff = w2.shape[1]
    sub = F32_SUBLANES
    idx_spec = lambda f: pl.BlockSpec((1, 1, rb), f, memory_space=pltpu.SMEM)
    dst_blocks = jnp.concatenate([n_rows + jnp.arange(rb, dtype=jnp.int32), dst]).reshape(n_blocks + 1, 1, rb)
    grid_spec = pltpu.PrefetchScalarGridSpec(
        num_scalar_prefetch=1,
        grid=(n_blocks,),
        in_specs=[
            idx_spec(lambda i, be: (i, 0, 0)),
            idx_spec(lambda i, be: (jnp.minimum(i + 1, n_blocks - 1), 0, 0)),
            idx_spec(lambda i, be: (i, 0, 0)),
            idx_spec(lambda i, be: (i + 1, 0, 0)),
            pl.BlockSpec(memory_space=pl.ANY),
            pl.BlockSpec((1, d, two_f), lambda i, be: (be[i], 0, 0), pipeline_mode=pl.Buffered(1)),
            pl.BlockSpec((1, d_ff, d), lambda i, be: (be[i], 0, 0), pipeline_mode=pl.Buffered(1)),
        ],
        out_specs=pl.BlockSpec(memory_space=pl.ANY),
        scratch_shapes=[
            pltpu.VMEM((2, rb * sub, LANES), F32),
            pltpu.VMEM((2, rb * sub, LANES), F32),
            pltpu.SemaphoreType.DMA((2,)),
            pltpu.SemaphoreType.DMA((2,)),
        ],
    )
    src3 = src.reshape(n_blocks, 1, rb)
    return pl.pallas_call(
        functools.partial(_expert_ffn_kernel, ff_tile=_ff_tile(d_ff, 512), n_blocks=n_blocks),
        grid_spec=grid_spec,
        out_shape=jax.ShapeDtypeStruct(((n_rows + rb) * sub, LANES), F32),
        compiler_params=_params("arbitrary"),
        name="expert_ffn",
    )(block_expert, src3, src3, dst_blocks, dst_blocks, x_tiles, w13, w2)


def _combine_kernel(y0_ref, y1_ref, route_ref, h_ref, mod_ref, gpost_ref, o_ref):
    route = route_ref[0]
    tm = route.shape[0]
    f = route[:, 2:3] * _load_row_tiles(y0_ref, tm) + route[:, 3:4] * _load_row_tiles(y1_ref, tm)
    o_ref[0] = h_ref[0] + mod_ref[0, 0][5:6] * _rms(f, gpost_ref[...])


def _combine(y_tiles, route, h, modtab, g_post, n_lat_tiles):
    bsz, rows, d = h.shape
    tm = TOKEN_TILE
    tpb = rows // tm
    second = bsz * tpb
    tok = lambda w: pl.BlockSpec((1, tm, w), lambda b, t: (b, t, 0))
    return pl.pallas_call(
        _combine_kernel,
        grid=(bsz, tpb),
        in_specs=[
            pl.BlockSpec((tm * F32_SUBLANES, LANES), lambda b, t: (b * tpb + t, 0)),
            pl.BlockSpec((tm * F32_SUBLANES, LANES), lambda b, t: (second + b * tpb + t, 0)),
            tok(LANES), tok(d),
            pl.BlockSpec((1, 1, N_MOD, d), lambda b, t: (b, jnp.where(t >= n_lat_tiles, 1, 0), 0, 0)),
            _resident(g_post.shape),
        ],
        out_specs=tok(d),
        out_shape=jax.ShapeDtypeStruct((bsz, rows, d), F32),
        compiler_params=_params("parallel", "parallel"),
        name="moe_combine",
    )(y_tiles, y_tiles, route, h, modtab, g_post)


def _moe_ffn(f_tiles, route, h, modtab, g_post, w13, w2, n_lat_tiles):
    bsz, rows, d = h.shape
    n_tok = bsz * rows
    n_exp = w13.shape[0]
    n_assign = n_tok * TOP_K
    rb = MOE_ROWS
    expert = route[..., :TOP_K].astype(jnp.int32).reshape(n_assign)
    counts = jnp.sum((expert[:, None] == jnp.arange(n_exp, dtype=jnp.int32)[None, :]).astype(jnp.int32), axis=0)
    ends = jnp.cumsum(counts)
    padded = (counts + rb - 1) // rb * rb
    padded_end = jnp.cumsum(padded)
    n_rows = n_assign + n_exp * rb
    n_blocks = n_rows // rb
    _, order = lax.sort((expert, jnp.arange(n_assign, dtype=jnp.int32)), num_keys=1, is_stable=True)
    block_start = jnp.arange(n_blocks, dtype=jnp.int32) * rb
    block_expert = jnp.minimum(
        jnp.sum((padded_end[None, :] <= block_start[:, None]).astype(jnp.int32), axis=1), n_exp - 1)
    row = block_start[:, None] + jnp.arange(rb, dtype=jnp.int32)[None, :]
    rank = row - (padded_end - padded)[block_expert][:, None]
    is_pad = rank >= counts[block_expert][:, None]
    held = order[jnp.clip((ends - counts)[block_expert][:, None] + rank, 0, n_assign - 1)]
    src = jnp.where(is_pad, 0, held // TOP_K).reshape(n_rows)
    spare = n_assign + row - ends[block_expert][:, None]
    dst = jnp.where(is_pad, spare, (held % TOP_K) * n_tok + held // TOP_K).reshape(n_rows)

    x_tiles = f_tiles.reshape(n_tok * F32_SUBLANES, LANES)
    y_tiles = _expert_ffn(x_tiles, block_expert, src, dst, w13, w2)
    return _combine(y_tiles, route, h, modtab, g_post, n_lat_tiles)


def _rot_cols(r):
    q = QK_ROPE // 4
    return jnp.concatenate([-r[..., q:2 * q], r[..., :q], -r[..., 3 * q:], r[..., 2 * q:3 * q]], axis=-1)


def _head_pad(nope, rope):
    pad = jnp.zeros(nope.shape[:-1] + (HEAD_PAD - QK_NOPE - QK_ROPE,), nope.dtype)
    out = jnp.concatenate([nope, rope, pad], axis=-1)
    return out.reshape(out.shape[:-2] + (out.shape[-2] * HEAD_PAD,))


def _value_pad(wv):
    k, n_heads, _ = wv.shape
    tiles = []
    for hd in range(n_heads):
        pos = hd % HEADS_PER_STEP
        tiles.append(jnp.pad(wv[:, hd], ((0, 0), (pos * V_HEAD, HEAD_PAD - (pos + 1) * V_HEAD))))
    return jnp.concatenate(tiles, axis=1)


def _layer_weights(layer, w_in, b_gate, conv_dw, conv_db, conv_ln_g, conv_ln_b, w_conv_out, q_norm_g, w_uq,
                   kv_norm_g, w_ukv, w_attn_out, gmlp_ln_g, gmlp_ln_b, w_spatial, b_spatial, w_gmlp_out,
                   w_out):
    d = w_in.shape[1]
    conv_dim = conv_dw.shape[2]
    q_lora, kv_lora = w_uq.shape[1], w_ukv.shape[1]
    gmlp_dim = w_gmlp_out.shape[1]
    col_q = 2 * conv_dim
    col_kv = col_q + q_lora
    col_kr = col_kv + kv_lora
    col_gm = col_kr + QK_ROPE
    col_gate = col_gm + 2 * gmlp_dim
    wi = w_in[layer]
    row = lambda v: v[layer].reshape(1, -1)

    uq = w_uq[layer].reshape(q_lora, MLA_HEADS, QK_NOPE + QK_ROPE)
    uq_nope, uq_rope = uq[..., :QK_NOPE], uq[..., QK_NOPE:]
    ukv = w_ukv[layer].reshape(kv_lora, MLA_HEADS, QK_NOPE + V_HEAD)
    wkr = wi[:, col_kr:col_gm]
    zeros_nope = jnp.zeros((d, 1, QK_NOPE), F32)
    groups = w_spatial.shape[1]
    return {
        "w_conv": wi[:, :col_q].astype(BF16),
        "w_cq": wi[:, col_q:col_kv].astype(BF16),
        "w_ckv": wi[:, col_kv:col_kr].astype(BF16),
        "w_kr": jnp.concatenate([_head_pad(zeros_nope, wkr[:, None, :]),
                                 _head_pad(zeros_nope, _rot_cols(wkr)[:, None, :])], axis=1).astype(BF16),
        "w_gm": wi[:, col_gm:col_gate].astype(BF16),
        "w_gate": wi[:, col_gate:].astype(BF16),
        "b_gate": row(b_gate),
        "q_norm_g": row(q_norm_g),
        "w_uq": _head_pad(uq_nope, uq_rope).astype(BF16),
        "w_uq_rot": _head_pad(jnp.zeros_like(uq_nope), _rot_cols(uq_rope)).astype(BF16),
        "kv_norm_g": row(kv_norm_g),
        "w_k": _head_pad(ukv[..., :QK_NOPE], jnp.zeros((kv_lora, MLA_HEADS, QK_ROPE), F32)).astype(BF16),
        "w_v": _value_pad(ukv[..., QK_NOPE:]).astype(BF16),
        "v_one": jnp.zeros((MLA_HEADS, HEAD_PAD), F32).at[
            jnp.arange(MLA_HEADS), jnp.array([_one_lane(hd) for hd in range(MLA_HEADS)])
        ].set(1.0).reshape(1, MLA_HEADS * HEAD_PAD),
        "gmlp_ln_g": row(gmlp_ln_g),
        "gmlp_ln_b": row(gmlp_ln_b),
        "conv_dw": conv_dw[layer],
        "conv_db": row(conv_db),
        "conv_ln_g": row(conv_ln_g),
        "conv_ln_b": row(conv_ln_b),
        "w_conv_out": w_conv_out[layer].astype(BF16),
        "w_attn_out": w_attn_out[layer].astype(BF16),
        "w_spatial": w_spatial[layer].astype(BF16),
        "b_spatial": jnp.repeat(b_spatial[layer].T, gmlp_dim // groups, axis=1),
        "w_gmlp_out": w_gmlp_out[layer].astype(BF16),
        "w_out": w_out[layer].astype(BF16),
    }


def _rope_tables(n_lat, n_ctx):
    pairs = QK_ROPE // 4
    pos = jnp.arange(n_lat, dtype=jnp.int32)
    row = (pos // GRID_W).astype(F32)
    col = (pos % GRID_W).astype(F32)
    inv_freq = ROPE_THETA ** (-jnp.arange(pairs, dtype=F32) / pairs)
    ar, ac = row[:, None] * inv_freq, col[:, None] * inv_freq

    def table(fn, fill):
        rope = jnp.concatenate([fn(ar), fn(ar), fn(ac), fn(ac)], axis=-1)
        lat = jnp.concatenate([jnp.full((n_lat, QK_NOPE), fill, F32), rope,
                               jnp.full((n_lat, HEAD_PAD - QK_NOPE - QK_ROPE), fill, F32)], axis=-1)
        return jnp.concatenate([lat, jnp.full((n_ctx, HEAD_PAD), fill, F32)], axis=0)

    return table(jnp.cos, 1.0), table(jnp.sin, 0.0)


def kernel(x, c, ctx, c_ctx, w_mod, b_mod, norm_g, w_in, b_gate, conv_dw, conv_db, conv_ln_g, conv_ln_b,
           w_conv_out, q_norm_g, w_uq, kv_norm_g, w_ukv, w_attn_out, gmlp_ln_g, gmlp_ln_b, w_spatial,
           b_spatial, w_gmlp_out, w_out, ffn_w13, ffn_w2, router_w, moe_w13, moe_w2):
    bsz, n_lat, d = x.shape
    n_ctx = ctx.shape[1]
    depth = w_mod.shape[0]
    tm = TOKEN_TILE
    assert n_lat % Q_TILE == 0 and n_lat % n_ctx == 0 and n_ctx % tm == 0 and n_lat % GRID_W == 0
    assert bsz + 1 <= MOD_ROWS and w_spatial.shape[2] == CHUNK and V_HEAD * HEADS_PER_STEP == LANES
    n_lat_tiles = n_lat // tm

    c_rows = jnp.zeros((MOD_ROWS, d), F32).at[:bsz].set(c).at[bsz].set(c_ctx)
    mod_all = _modulation(c_rows, w_mod, b_mod)
    cos_t, sin_t = _rope_tables(n_lat, n_ctx)
    h = jnp.concatenate([x, ctx], axis=1)

    for layer in range(depth):
        need_ctx = layer < depth - 1
        rows_out = n_lat + n_ctx if need_ctx else n_lat
        mod = mod_all[layer].reshape(MOD_ROWS, N_MOD, d)
        modtab = jnp.stack([mod[:bsz], jnp.broadcast_to(mod[bsz], (bsz, N_MOD, d))], axis=1)
        g = norm_g[layer].reshape(4, 1, d)
        wts = _layer_weights(layer, w_in, b_gate, conv_dw, conv_db, conv_ln_g, conv_ln_b, w_conv_out,
                             q_norm_g, w_uq, kv_norm_g, w_ukv, w_attn_out, gmlp_ln_g, gmlp_ln_b, w_spatial,
                             b_spatial, w_gmlp_out, w_out)

        u, q, k, v, gmu, gmv, gates = _inproj(h, modtab, g[0], cos_t, sin_t, wts, n_lat_tiles)
        o_lat = _attention(q, k, v, 0, n_lat, 0, n_lat + n_ctx)
        o_ctx = _attention(q, k, v, n_lat, n_ctx, n_lat, n_ctx) if need_ctx else None

        idx = layer // 2
        router = None
        if layer % 2 == 1:
            n_exp = router_w.shape[2]
            wr = jnp.zeros((d, LANES), F32).at[:, :n_exp].set(router_w[idx])
            wr_hi = wr.astype(BF16)
            router = (wr_hi, (wr - wr_hi.astype(F32)).astype(BF16), n_exp)
        outs = _mixer(h, modtab, u, o_lat, o_ctx, gmu, gmv, gates, wts, g[1], g[2], n_lat_tiles, rows_out,
                      router)
        if router is None:
            h_mid, f_in = outs
            h = _dense_ffn(f_in, h_mid, modtab, g[3], ffn_w13[idx].astype(BF16)[None],
                           ffn_w2[idx].astype(BF16)[None], n_lat_tiles)
        else:
            h_mid, f_in, route = outs
            h = _moe_ffn(f_in, route, h_mid, modtab, g[3], moe_w13[idx].astype(BF16),
                         moe_w2[idx].astype(BF16), n_lat_tiles)
    return h[:, :n_lat]
```

```python
import functools
import math

import jax
import jax.numpy as jnp
from jax import lax
from jax.experimental import pallas as pl
from jax.experimental.pallas import tpu as pltpu

F32 = jnp.float32
BF16 = jnp.bfloat16

GRID_W = 64
MLA_HEADS = 8
QK_NOPE = 64
QK_ROPE = 32
V_HEAD = 64
ROPE_THETA = 10000.0
CHUNK = 128
TOP_K = 2
N_MOD = 6
RMS_EPS = 1e-6
LN_EPS = 1e-5

LANES = 128
F32_SUBLANES = 8
BF16_SUBLANES = 16
VMEM_LIMIT_BYTES = 56 * 1024 * 1024

HEAD_PAD = LANES
HEADS_PER_STEP = LANES // V_HEAD
TOKEN_TILE = 256
HALO = BF16_SUBLANES
CONV_ROWS = 128
Q_TILE = 512
KV_TILE = 256
MOE_ROWS = 512
MOD_ROWS = 16


def _params(*sem):
    return pltpu.CompilerParams(dimension_semantics=sem, vmem_limit_bytes=VMEM_LIMIT_BYTES)


def _resident(shape):
    zeros = (0,) * len(shape)
    return pl.BlockSpec(shape, lambda *_: zeros, pipeline_mode=pl.Buffered(1))


def _dot(a, b):
    return jnp.dot(a, b, preferred_element_type=F32)


def _split_bf16(x):
    hi = x.astype(BF16)
    lo = (x - hi.astype(F32)).astype(BF16)
    return hi, lo


def _dot_f32(x, w_hi, w_lo):
    x_hi, x_lo = _split_bf16(x)
    return _dot(x_hi, w_hi) + (_dot(x_lo, w_hi) + _dot(x_hi, w_lo))


def _rms(x, g):
    return x * lax.rsqrt(jnp.mean(x * x, axis=-1, keepdims=True) + RMS_EPS) * g


def _layer_norm(x, g, b):
    xc = x - jnp.mean(x, axis=-1, keepdims=True)
    var = jnp.mean(xc * xc, axis=-1, keepdims=True)
    return xc * lax.rsqrt(var + LN_EPS) * g + b


def _silu(x):
    return x * jax.nn.sigmoid(x)


def _load_row_tiles(ref, rows):
    return jnp.concatenate([ref[pl.ds(s, rows, stride=F32_SUBLANES), :] for s in range(F32_SUBLANES)], axis=1)


def _store_row_tiles(ref, x):
    for s in range(F32_SUBLANES):
        ref[pl.ds(s, x.shape[0], stride=F32_SUBLANES), :] = x[:, s * LANES:(s + 1) * LANES]


def _mod_kernel(c_ref, w_ref, b_ref, o_ref):
    w_hi, w_lo = _split_bf16(w_ref[0])
    o_ref[0] = _dot_f32(_silu(c_ref[...]), w_hi, w_lo) + b_ref[0]


def _modulation(c_rows, w_mod, b_mod):
    n_layers, d, n_out = w_mod.shape
    return pl.pallas_call(
        _mod_kernel,
        grid=(n_layers, n_out // d),
        in_specs=[
            pl.BlockSpec((MOD_ROWS, d), lambda l, j: (0, 0)),
            pl.BlockSpec((1, d, d), lambda l, j: (l, 0, j)),
            pl.BlockSpec((1, 1, d), lambda l, j: (l, 0, j)),
        ],
        out_specs=pl.BlockSpec((1, MOD_ROWS, d), lambda l, j: (l, 0, j)),
        out_shape=jax.ShapeDtypeStruct((n_layers, MOD_ROWS, n_out), F32),
        compiler_params=_params("arbitrary", "arbitrary"),
        name="modulation",
    )(c_rows, w_mod, b_mod.reshape(n_layers, 1, n_out))


def _depthwise_conv(win_ref, dw_ref, rows, chunk):
    n_taps = dw_ref.shape[0]
    first = HALO - n_taps // 2
    sub = F32_SUBLANES
    span = CONV_ROWS + (first + n_taps - 1) // sub * sub + sub
    cs = slice(chunk * LANES, (chunk + 1) * LANES)
    blocks = []
    for r0 in range(0, rows, CONV_ROWS):
        acc = jnp.zeros((CONV_ROWS, LANES), F32)
        window = win_ref[r0:r0 + span, cs]
        for phase in range(sub):
            taps = [kk for kk in range(n_taps) if (first + kk) % sub == phase]
            if taps:
                ws = window if phase == 0 else pltpu.roll(window, span - phase, axis=0)
                for kk in taps:
                    a = (first + kk) // sub * sub
                    acc = acc + ws[a:a + CONV_ROWS] * dw_ref[kk:kk + 1, cs]
        blocks.append(acc)
    return jnp.concatenate(blocks, axis=0)


def _inproj_kernel(h_ref, hprev_ref, hnext_ref, mod_ref, gpre_ref, cos_ref, sin_ref,
                   w_conv_ref, w_cq_ref, w_ckv_ref, w_kr_ref, w_gm_ref, w_gate_ref, b_gate_ref,
                   qg_ref, wuq_ref, wuq_rot_ref, kvg_ref, wk_ref, wv_ref, vone_ref, gln_g_ref, gln_b_ref,
                   dw_ref, db_ref, cln_g_ref, cln_b_ref,
                   cact_ref, q_ref, k_ref, v_ref, gmu_ref, gmv_ref, gate_ref, win_ref, *,
                   scale, n_lat_tiles, n_tiles):
    t = pl.program_id(1)
    tm = h_ref.shape[1]
    mod = mod_ref[0, 0]

    def modulated(hx):
        return (_rms(hx, gpre_ref[...]) * (1.0 + mod[1:2]) + mod[0:1]).astype(BF16)

    a = modulated(h_ref[0])
    cos = cos_ref[...]
    sin = sin_ref[...]

    a_halo = jnp.concatenate([modulated(hprev_ref[0]), a, modulated(hnext_ref[0])], axis=0)
    zc = _dot(a_halo, w_conv_ref[...])
    half = zc.shape[1] // 2
    u = zc[:, :half] * jax.nn.sigmoid(zc[:, half:])
    prev_ok = jnp.logical_and(t != 0, t != n_lat_tiles)
    next_ok = jnp.logical_and(t != n_lat_tiles - 1, t != n_tiles - 1)
    win_ref[0:HALO, :] = jnp.where(prev_ok, u[0:HALO], 0.0)
    win_ref[HALO:HALO + tm, :] = u[HALO:HALO + tm]
    win_ref[HALO + tm:, :] = jnp.where(next_ok, u[HALO + tm:], 0.0)
    conv = [_depthwise_conv(win_ref, dw_ref, tm, c) for c in range(dw_ref.shape[1] // LANES)]
    conv = jnp.concatenate(conv, axis=-1) + db_ref[...]
    cact_ref[0] = _silu(_layer_norm(conv, cln_g_ref[...], cln_b_ref[...])).astype(BF16)

    cq = _rms(_dot(a, w_cq_ref[...]), qg_ref[...]).astype(BF16)
    qf = _dot(cq, wuq_ref[...])
    qr = _dot(cq, wuq_rot_ref[...])
    for hd in range(qf.shape[1] // HEAD_PAD):
        sl = slice(hd * HEAD_PAD, (hd + 1) * HEAD_PAD)
        q_ref[0, :, sl] = ((qf[:, sl] * cos + qr[:, sl] * sin) * scale).astype(BF16)

    ckv = _rms(_dot(a, w_ckv_ref[...]), kvg_ref[...]).astype(BF16)
    kr = _dot(a, w_kr_ref[...])
    kr = kr[:, :HEAD_PAD] * cos + kr[:, HEAD_PAD:] * sin
    kn = _dot(ckv, wk_ref[...])
    for hd in range(kn.shape[1] // HEAD_PAD):
        sl = slice(hd * HEAD_PAD, (hd + 1) * HEAD_PAD)
        k_ref[0, :, sl] = (kn[:, sl] + kr).astype(BF16)
    v_ref[0] = (_dot(ckv, wv_ref[...]) + vone_ref[...]).astype(BF16)

    zg = jax.nn.gelu(_dot(a, w_gm_ref[...]), approximate=True)
    gd = zg.shape[1] // 2
    gmu_ref[0] = zg[:, :gd].astype(BF16)
    gmv_ref[0] = _layer_norm(zg[:, gd:], gln_g_ref[...], gln_b_ref[...]).astype(BF16)

    gate_ref[0] = jax.nn.sigmoid(_dot(a, w_gate_ref[...]) + b_gate_ref[...]).astype(BF16)


def _inproj(h, modtab, g_pre, cos_t, sin_t, wts, n_lat_tiles):
    bsz, nt, d = h.shape
    tm = TOKEN_TILE
    tok = lambda w: pl.BlockSpec((1, tm, w), lambda b, t: (b, t, 0))
    weights = [wts[k] for k in ("w_conv", "w_cq", "w_ckv", "w_kr", "w_gm", "w_gate", "b_gate", "q_norm_g",
                                "w_uq", "w_uq_rot", "kv_norm_g", "w_k", "w_v", "v_one", "gmlp_ln_g",
                                "gmlp_ln_b", "conv_dw", "conv_db", "conv_ln_g", "conv_ln_b")]
    conv_dim = wts["w_conv"].shape[1] // 2
    widths = (conv_dim, wts["w_uq"].shape[1], wts["w_k"].shape[1], wts["w_v"].shape[1],
              wts["w_gm"].shape[1] // 2, wts["w_gm"].shape[1] // 2, wts["w_gate"].shape[1])
    scale = math.log2(math.e) / math.sqrt(QK_NOPE + QK_ROPE)
    hb = tm // HALO
    last_halo = nt // HALO - 1
    return pl.pallas_call(
        functools.partial(_inproj_kernel, scale=scale, n_lat_tiles=n_lat_tiles, n_tiles=nt // tm),
        grid=(bsz, nt // tm),
        in_specs=[
            tok(d),
            pl.BlockSpec((1, HALO, d), lambda b, t: (b, jnp.maximum(t * hb - 1, 0), 0)),
            pl.BlockSpec((1, HALO, d), lambda b, t: (b, jnp.minimum((t + 1) * hb, last_halo), 0)),
            pl.BlockSpec((1, 1, N_MOD, d), lambda b, t: (b, jnp.where(t >= n_lat_tiles, 1, 0), 0, 0)),
            _resident(g_pre.shape),
            pl.BlockSpec((tm, LANES), lambda b, t: (t, 0)),
            pl.BlockSpec((tm, LANES), lambda b, t: (t, 0)),
        ] + [_resident(w.shape) for w in weights],
        out_specs=[tok(w) for w in widths],
        out_shape=[jax.ShapeDtypeStruct((bsz, nt, w), BF16) for w in widths],
        scratch_shapes=[pltpu.VMEM((tm + 2 * HALO, conv_dim), F32)],
        compiler_params=_params("parallel", "parallel"),
        name="inproj",
    )(h, h, h, modtab, g_pre, cos_t, sin_t, *weights)


def _one_lane(head):
    return ((head % HEADS_PER_STEP + 1) % HEADS_PER_STEP) * V_HEAD


def _attn_kernel(q_ref, k_ref, v_ref, o_ref, *, kv_tile):
    tq = q_ref.shape[1]
    n_kv = k_ref.shape[1] // kv_tile
    heads = [slice(hd * HEAD_PAD, (hd + 1) * HEAD_PAD) for hd in range(HEADS_PER_STEP)]
    qs = [q_ref[0, :, hs] for hs in heads]

    def body(j, carry):
        rows = pl.ds(pl.multiple_of(j * kv_tile, kv_tile), kv_tile)
        out = []
        for hs, q, (m, acc) in zip(heads, qs, carry):
            s = lax.dot_general(q, k_ref[0, rows, hs], (((1,), (1,)), ((), ())),
                                preferred_element_type=F32)
            m_new = jnp.maximum(m, jnp.max(s, axis=-1, keepdims=True))
            p = jnp.exp2(s - m_new).astype(BF16)
            acc = jnp.exp2(m - m_new) * acc + _dot(p, v_ref[0, rows, hs])
            out.append((m_new, acc))
        return tuple(out)

    init = tuple((jnp.full((tq, 1), -1e30, F32), jnp.zeros((tq, LANES), F32)) for _ in heads)
    carry = lax.fori_loop(0, n_kv, body, init, unroll=True)
    lane = lax.broadcasted_iota(jnp.int32, (tq, LANES), 1)
    o = None
    for hd in range(HEADS_PER_STEP - 1, -1, -1):
        acc = carry[hd][1]
        one = _one_lane(hd)
        o_hd = acc / acc[:, one:one + 1]
        o = o_hd if o is None else jnp.where(lane < (hd + 1) * V_HEAD, o_hd, o)
    o_ref[0] = o.astype(BF16)


def _attention(q, k, v, q_start, q_len, kv_start, kv_len):
    bsz = q.shape[0]
    qw = HEADS_PER_STEP * HEAD_PAD
    n_pairs = v.shape[2] // qw
    tq = min(Q_TILE, q_len)
    kv_tile = KV_TILE if kv_len % KV_TILE == 0 else math.gcd(kv_len, KV_TILE)
    q_off, kv_blk = q_start // tq, kv_start // kv_len
    return pl.pallas_call(
        functools.partial(_attn_kernel, kv_tile=kv_tile),
        grid=(bsz, n_pairs, q_len // tq),
        in_specs=[
            pl.BlockSpec((1, tq, qw), lambda b, p, i: (b, q_off + i, p)),
            pl.BlockSpec((1, kv_len, qw), lambda b, p, i: (b, kv_blk, p)),
            pl.BlockSpec((1, kv_len, qw), lambda b, p, i: (b, kv_blk, p)),
        ],
        out_specs=pl.BlockSpec((1, tq, LANES), lambda b, p, i: (b, i, p)),
        out_shape=jax.ShapeDtypeStruct((bsz, q_len, n_pairs * LANES), BF16),
        compiler_params=_params("parallel", "parallel", "arbitrary"),
        name="attention",
    )(q, k, v)


def _mixer_kernel(*refs, n_lat_tiles, has_ctx, moe_next, n_experts):
    (h_ref, mod_ref, cact_ref, ol_ref, oc_ref, gmu_ref, gmv_ref, gate_ref,
     wco_ref, wao_ref, ws_ref, bs_ref, wgo_ref, wout_ref, gpost_ref, gpre_ref) = refs[:16]
    rest = refs[16:]
    if moe_next:
        wr_hi_ref, wr_lo_ref, hout_ref, fin_ref, route_ref = rest
    else:
        hout_ref, fin_ref = rest
    t = pl.program_id(1)
    tm = h_ref.shape[1]
    d = h_ref.shape[2]
    mod = mod_ref[0, 0]

    br_conv = _dot(cact_ref[0], wco_ref[...])

    o = ol_ref[0]
    if has_ctx:
        o = jnp.where(t >= n_lat_tiles, oc_ref[0], o)
    br_attn = _dot(o, wao_ref[...])

    gmv = gmv_ref[0]
    lane = lax.broadcasted_iota(jnp.int32, (CHUNK, LANES), 1)
    group_dim = gmv.shape[1] // ws_ref.shape[0]
    rows = []
    for j in range(tm // CHUNK):
        cols = []
        for gp in range(gmv.shape[1] // LANES):
            vp = gmv[j * CHUNK:(j + 1) * CHUNK, gp * LANES:(gp + 1) * LANES]
            s0 = _dot(ws_ref[2 * gp], vp)
            s1 = _dot(ws_ref[2 * gp + 1], vp)
            cols.append(jnp.where(lane < group_dim, s0, s1))
        rows.append(jnp.concatenate(cols, axis=-1) + bs_ref[...])
    gm = (gmu_ref[0].astype(F32) * jnp.concatenate(rows, axis=0)).astype(BF16)
    br_gmlp = _dot(gm, wgo_ref[...])

    g = gate_ref[0]
    mix = (g[:, :d] * br_conv.astype(BF16) + g[:, d:2 * d] * br_attn.astype(BF16)
           + g[:, 2 * d:] * br_gmlp.astype(BF16))
    y = _dot(mix, wout_ref[...])
    h_new = h_ref[0] + mod[2:3] * _rms(y, gpost_ref[...])
    hout_ref[0] = h_new
    f_in = _rms(h_new, gpre_ref[...]) * (1.0 + mod[4:5]) + mod[3:4]
    if not moe_next:
        fin_ref[0] = f_in.astype(fin_ref.dtype)
    else:
        _store_row_tiles(fin_ref.at[0], f_in)
        logits = _dot_f32(f_in, wr_hi_ref[...], wr_lo_ref[...])
        ln = lax.broadcasted_iota(jnp.int32, logits.shape, 1)
        neg = -jnp.inf
        lg = jnp.where(ln < n_experts, logits, neg)
        m1 = jnp.max(lg, axis=-1, keepdims=True)
        i1 = jnp.min(jnp.where(lg == m1, ln, LANES), axis=-1, keepdims=True)
        lg2 = jnp.where(ln == i1, neg, lg)
        m2 = jnp.max(lg2, axis=-1, keepdims=True)
        i2 = jnp.min(jnp.where(lg2 == m2, ln, LANES), axis=-1, keepdims=True)
        e2 = jnp.exp(m2 - m1)
        w1 = 1.0 / (1.0 + e2)
        w2 = e2 / (1.0 + e2)
        route = jnp.where(ln == 0, i1.astype(F32),
                          jnp.where(ln == 1, i2.astype(F32),
                                    jnp.where(ln == 2, w1, jnp.where(ln == 3, w2, 0.0))))
        route_ref[0] = route


def _mixer(h, modtab, c_act, o_lat, o_ctx, gmu, gmv, gates, wts, g_post, g_pre_f, n_lat_tiles, rows_out,
           router):
    bsz, nt, d = h.shape
    tm = TOKEN_TILE
    has_ctx = o_ctx is not None
    moe_next = router is not None
    if not has_ctx:
        o_ctx = o_lat
    n_ctx_blocks = o_ctx.shape[1] // tm
    tok = lambda w: pl.BlockSpec((1, tm, w), lambda b, t: (b, t, 0))
    weights = [wts[k] for k in ("w_conv_out", "w_attn_out", "w_spatial", "b_spatial", "w_gmlp_out",
                                "w_out")] + [g_post, g_pre_f]
    in_specs = [
        tok(d),
        pl.BlockSpec((1, 1, N_MOD, d), lambda b, t: (b, jnp.where(t >= n_lat_tiles, 1, 0), 0, 0)),
        tok(c_act.shape[2]),
        pl.BlockSpec((1, tm, o_lat.shape[2]), lambda b, t: (b, jnp.minimum(t, n_lat_tiles - 1), 0)),
        pl.BlockSpec((1, tm, o_ctx.shape[2]),
                     lambda b, t: (b, jnp.clip(t - n_lat_tiles, 0, n_ctx_blocks - 1), 0)),
        tok(gmu.shape[2]), tok(gmv.shape[2]), tok(gates.shape[2]),
    ] + [_resident(w.shape) for w in weights]
    operands = [h, modtab, c_act, o_lat, o_ctx, gmu, gmv, gates] + weights
    out_specs = [tok(d), tok(d)]
    out_shape = [jax.ShapeDtypeStruct((bsz, rows_out, d), F32), jax.ShapeDtypeStruct((bsz, rows_out, d), BF16)]
    n_experts = 0
    if moe_next:
        assert d == F32_SUBLANES * LANES
        out_specs[1] = pl.BlockSpec((1, tm * F32_SUBLANES, LANES), lambda b, t: (b, t, 0))
        out_shape[1] = jax.ShapeDtypeStruct((bsz, rows_out * F32_SUBLANES, LANES), F32)
        n_experts = router[2]
        in_specs += [_resident(router[0].shape), _resident(router[1].shape)]
        operands += [router[0], router[1]]
        out_specs.append(tok(LANES))
        out_shape.append(jax.ShapeDtypeStruct((bsz, rows_out, LANES), F32))
    return pl.pallas_call(
        functools.partial(_mixer_kernel, n_lat_tiles=n_lat_tiles, has_ctx=has_ctx, moe_next=moe_next,
                          n_experts=n_experts),
        grid=(bsz, rows_out // tm),
        in_specs=in_specs,
        out_specs=out_specs,
        out_shape=out_shape,
        compiler_params=_params("parallel", "parallel"),
        name="mixer",
    )(*operands)


def _swiglu_block(x, w13_ref, w2_ref, ff_tile, after_matmul=None):
    d_ff = w2_ref.shape[1]
    n_chunks = d_ff // ff_tile
    done = [0]

    def dot_then_side_work(a, b):
        out = _dot(a, b)
        if after_matmul is not None:
            after_matmul(done[0], 3 * n_chunks)
        done[0] += 1
        return out

    acc = jnp.zeros((x.shape[0], w2_ref.shape[2]), F32)
    for j in range(n_chunks):
        a = dot_then_side_work(x, w13_ref[0, :, j * ff_tile:(j + 1) * ff_tile])
        g = dot_then_side_work(x, w13_ref[0, :, d_ff + j * ff_tile:d_ff + (j + 1) * ff_tile])
        acc = acc + dot_then_side_work((_silu(a) * g).astype(BF16), w2_ref[0, j * ff_tile:(j + 1) * ff_tile, :])
    return acc


def _dense_ffn_kernel(x_ref, h_ref, mod_ref, gpost_ref, w13_ref, w2_ref, o_ref, *, ff_tile):
    f = _swiglu_block(x_ref[0], w13_ref, w2_ref, ff_tile)
    o_ref[0] = h_ref[0] + mod_ref[0, 0][5:6] * _rms(f, gpost_ref[...])


def _ff_tile(d_ff, target):
    n = d_ff // LANES
    best = 1
    for c in range(1, n + 1):
        if n % c == 0 and c * LANES <= target:
            best = c
    return best * LANES


def _dense_ffn(f_in, h, modtab, g_post, w13, w2, n_lat_tiles):
    bsz, rows, d = h.shape
    tm = TOKEN_TILE
    tok = lambda w: pl.BlockSpec((1, tm, w), lambda b, t: (b, t, 0))
    return pl.pallas_call(
        functools.partial(_dense_ffn_kernel, ff_tile=_ff_tile(w2.shape[1], 1536)),
        grid=(bsz, rows // tm),
        in_specs=[
            tok(d), tok(d),
            pl.BlockSpec((1, 1, N_MOD, d), lambda b, t: (b, jnp.where(t >= n_lat_tiles, 1, 0), 0, 0)),
            _resident(g_post.shape), _resident(w13.shape), _resident(w2.shape),
        ],
        out_specs=tok(d),
        out_shape=jax.ShapeDtypeStruct((bsz, rows, d), F32),
        compiler_params=_params("parallel", "parallel"),
        name="dense_ffn",
    )(f_in, h, modtab, g_post, w13, w2)


def _expert_ffn_kernel(be_ref, src_ref, src_next_ref, dst_prev_ref, dst_ref, x_hbm, w13_ref, w2_ref, y_hbm,
                       xbuf, ybuf, gsem, ssem, *, ff_tile, n_blocks):
    del be_ref
    i = pl.program_id(0)
    rows = src_ref.shape[2]
    slot = i % 2
    other = 1 - slot
    sub = F32_SUBLANES

    def gather_row(idx_ref, buf_slot, r):
        pltpu.make_async_copy(x_hbm.at[pl.ds(idx_ref[0, 0, r] * sub, sub)],
                              xbuf.at[buf_slot, pl.ds(r * sub, sub)], gsem.at[buf_slot]).start()

    def scatter_row(idx_ref, buf_slot, r):
        pltpu.make_async_copy(ybuf.at[buf_slot, pl.ds(r * sub, sub)],
                              y_hbm.at[pl.ds(idx_ref[0, 0, r] * sub, sub)], ssem.at[buf_slot]).start()

    def all_rows(row_fn):
        def body(r, carry):
            row_fn(r)
            return carry
        lax.fori_loop(0, rows, body, 0)

    def wait_rows(hbm, buf, sem, buf_slot):
        pltpu.make_async_copy(hbm.at[pl.ds(0, rows * sub)], buf.at[buf_slot], sem.at[buf_slot]).wait()

    @pl.when(i == 0)
    def _():
        all_rows(functools.partial(gather_row, src_ref, 0))
        ybuf[1] = jnp.zeros(ybuf.shape[1:], F32)

    wait_rows(x_hbm, xbuf, gsem, slot)

    def side_copies(k, n):
        per = -(-rows // n)
        for r in range(k * per, min((k + 1) * per, rows)):
            gather_row(src_next_ref, other, r)
            scatter_row(dst_prev_ref, other, r)

    x = _load_row_tiles(xbuf.at[slot], rows).astype(BF16)
    y = _swiglu_block(x, w13_ref, w2_ref, ff_tile, side_copies)

    @pl.when(i >= 1)
    def _():
        wait_rows(y_hbm, ybuf, ssem, slot)

    _store_row_tiles(ybuf.at[slot], y)

    @pl.when(i == n_blocks - 1)
    def _():
        all_rows(functools.partial(scatter_row, dst_ref, slot))
        wait_rows(x_hbm, xbuf, gsem, other)
        wait_rows(y_hbm, ybuf, ssem, other)
        wait_rows(y_hbm, ybuf, ssem, slot)


def _expert_ffn(x_tiles, block_expert, src, dst, w13, w2):
    n_rows = src.shape[0]
    rb = MOE_ROWS
    n_blocks = n_rows // rb
    _, d, two_f = w13.shape
    d_ff = w2.shape[1]
    sub = F32_SUBLANES
    idx_spec = lambda f: pl.BlockSpec((1, 1, rb), f, memory_space=pltpu.SMEM)
    dst_blocks = jnp.concatenate([n_rows + jnp.arange(rb, dtype=jnp.int32), dst]).reshape(n_blocks + 1, 1, rb)
    grid_spec = pltpu.PrefetchScalarGridSpec(
        num_scalar_prefetch=1,
        grid=(n_blocks,),
        in_specs=[
            idx_spec(lambda i, be: (i, 0, 0)),
            idx_spec(lambda i, be: (jnp.minimum(i + 1, n_blocks - 1), 0, 0)),
            idx_spec(lambda i, be: (i, 0, 0)),
            idx_spec(lambda i, be: (i + 1, 0, 0)),
            pl.BlockSpec(memory_space=pl.ANY),
            pl.BlockSpec((1, d, two_f), lambda i, be: (be[i], 0, 0), pipeline_mode=pl.Buffered(1)),
            pl.BlockSpec((1, d_ff, d), lambda i, be: (be[i], 0, 0), pipeline_mode=pl.Buffered(1)),
        ],
        out_specs=pl.BlockSpec(memory_space=pl.ANY),
        scratch_shapes=[
            pltpu.VMEM((2, rb * sub, LANES), F32),
            pltpu.VMEM((2, rb * sub, LANES), F32),
            pltpu.SemaphoreType.DMA((2,)),
            pltpu.SemaphoreType.DMA((2,)),
        ],
    )
    src3 = src.reshape(n_blocks, 1, rb)
    return pl.pallas_call(
        functools.partial(_expert_ffn_kernel, ff_tile=_ff_tile(d_ff, 512), n_blocks=n_blocks),
        grid_spec=grid_spec,
        out_shape=jax.ShapeDtypeStruct(((n_rows + rb) * sub, LANES), F32),
        compiler_params=_params("arbitrary"),
        name="expert_ffn",
    )(block_expert, src3, src3, dst_blocks, dst_blocks, x_tiles, w13, w2)


def _combine_kernel(y0_ref, y1_ref, route_ref, h_ref, mod_ref, gpost_ref, o_ref):
    route = route_ref[0]
    tm = route.shape[0]
    f = route[:, 2:3] * _load_row_tiles(y0_ref, tm) + route[:, 3:4] * _load_row_tiles(y1_ref, tm)
    o_ref[0] = h_ref[0] + mod_ref[0, 0][5:6] * _rms(f, gpost_ref[...])


def _combine(y_tiles, route, h, modtab, g_post, n_lat_tiles):
    bsz, rows, d = h.shape
    tm = TOKEN_TILE
    tpb = rows // tm
    second = bsz * tpb
    tok = lambda w: pl.BlockSpec((1, tm, w), lambda b, t: (b, t, 0))
    return pl.pallas_call(
        _combine_kernel,
        grid=(bsz, tpb),
        in_specs=[
            pl.BlockSpec((tm * F32_SUBLANES, LANES), lambda b, t: (b * tpb + t, 0)),
            pl.BlockSpec((tm * F32_SUBLANES, LANES), lambda b, t: (second + b * tpb + t, 0)),
            tok(LANES), tok(d),
            pl.BlockSpec((1, 1, N_MOD, d), lambda b, t: (b, jnp.where(t >= n_lat_tiles, 1, 0), 0, 0)),
            _resident(g_post.shape),
        ],
        out_specs=tok(d),
        out_shape=jax.ShapeDtypeStruct((bsz, rows, d), F32),
        compiler_params=_params("parallel", "parallel"),
        name="moe_combine",
    )(y_tiles, y_tiles, route, h, modtab, g_post)


def _moe_ffn(f_tiles, route, h, modtab, g_post, w13, w2, n_lat_tiles):
    bsz, rows, d = h.shape
    n_tok = bsz * rows
    n_exp = w13.shape[0]
    n_assign = n_tok * TOP_K
    rb = MOE_ROWS
    expert = route[..., :TOP_K].astype(jnp.int32).reshape(n_assign)
    counts = jnp.sum((expert[:, None] == jnp.arange(n_exp, dtype=jnp.int32)[None, :]).astype(jnp.int32), axis=0)
    ends = jnp.cumsum(counts)
    padded = (counts + rb - 1) // rb * rb
    padded_end = jnp.cumsum(padded)
    n_rows = n_assign + n_exp * rb
    n_blocks = n_rows // rb
    _, order = lax.sort((expert, jnp.arange(n_assign, dtype=jnp.int32)), num_keys=1, is_stable=True)
    block_start = jnp.arange(n_blocks, dtype=jnp.int32) * rb
    block_expert = jnp.minimum(
        jnp.sum((padded_end[None, :] <= block_start[:, None]).astype(jnp.int32), axis=1), n_exp - 1)
    row = block_start[:, None] + jnp.arange(rb, dtype=jnp.int32)[None, :]
    rank = row - (padded_end - padded)[block_expert][:, None]
    is_pad = rank >= counts[block_expert][:, None]
    held = order[jnp.clip((ends - counts)[block_expert][:, None] + rank, 0, n_assign - 1)]
    src = jnp.where(is_pad, 0, held // TOP_K).reshape(n_rows)
    spare = n_assign + row - ends[block_expert][:, None]
    dst = jnp.where(is_pad, spare, (held % TOP_K) * n_tok + held // TOP_K).reshape(n_rows)

    x_tiles = f_tiles.reshape(n_tok * F32_SUBLANES, LANES)
    y_tiles = _expert_ffn(x_tiles, block_expert, src, dst, w13, w2)
    return _combine(y_tiles, route, h, modtab, g_post, n_lat_tiles)


def _rot_cols(r):
    q = QK_ROPE // 4
    return jnp.concatenate([-r[..., q:2 * q], r[..., :q], -r[..., 3 * q:], r[..., 2 * q:3 * q]], axis=-1)


def _head_pad(nope, rope):
    pad = jnp.zeros(nope.shape[:-1] + (HEAD_PAD - QK_NOPE - QK_ROPE,), nope.dtype)
    out = jnp.concatenate([nope, rope, pad], axis=-1)
    return out.reshape(out.shape[:-2] + (out.shape[-2] * HEAD_PAD,))


def _value_pad(wv):
    k, n_heads, _ = wv.shape
    tiles = []
    for hd in range(n_heads):
        pos = hd % HEADS_PER_STEP
        tiles.append(jnp.pad(wv[:, hd], ((0, 0), (pos * V_HEAD, HEAD_PAD - (pos + 1) * V_HEAD))))
    return jnp.concatenate(tiles, axis=1)


def _layer_weights(layer, w_in, b_gate, conv_dw, conv_db, conv_ln_g, conv_ln_b, w_conv_out, q_norm_g, w_uq,
                   kv_norm_g, w_ukv, w_attn_out, gmlp_ln_g, gmlp_ln_b, w_spatial, b_spatial, w_gmlp_out,
                   w_out):
    d = w_in.shape[1]
    conv_dim = conv_dw.shape[2]
    q_lora, kv_lora = w_uq.shape[1], w_ukv.shape[1]
    gmlp_dim = w_gmlp_out.shape[1]
    col_q = 2 * conv_dim
    col_kv = col_q + q_lora
    col_kr = col_kv + kv_lora
    col_gm = col_kr + QK_ROPE
    col_gate = col_gm + 2 * gmlp_dim
    wi = w_in[layer]
    row = lambda v: v[layer].reshape(1, -1)

    uq = w_uq[layer].reshape(q_lora, MLA_HEADS, QK_NOPE + QK_ROPE)
    uq_nope, uq_rope = uq[..., :QK_NOPE], uq[..., QK_NOPE:]
    ukv = w_ukv[layer].reshape(kv_lora, MLA_HEADS, QK_NOPE + V_HEAD)
    wkr = wi[:, col_kr:col_gm]
    zeros_nope = jnp.zeros((d, 1, QK_NOPE), F32)
    groups = w_spatial.shape[1]
    return {
        "w_conv": wi[:, :col_q].astype(BF16),
        "w_cq": wi[:, col_q:col_kv].astype(BF16),
        "w_ckv": wi[:, col_kv:col_kr].astype(BF16),
        "w_kr": jnp.concatenate([_head_pad(zeros_nope, wkr[:, None, :]),
                                 _head_pad(zeros_nope, _rot_cols(wkr)[:, None, :])], axis=1).astype(BF16),
        "w_gm": wi[:, col_gm:col_gate].astype(BF16),
        "w_gate": wi[:, col_gate:].astype(BF16),
        "b_gate": row(b_gate),
        "q_norm_g": row(q_norm_g),
        "w_uq": _head_pad(uq_nope, uq_rope).astype(BF16),
        "w_uq_rot": _head_pad(jnp.zeros_like(uq_nope), _rot_cols(uq_rope)).astype(BF16),
        "kv_norm_g": row(kv_norm_g),
        "w_k": _head_pad(ukv[..., :QK_NOPE], jnp.zeros((kv_lora, MLA_HEADS, QK_ROPE), F32)).astype(BF16),
        "w_v": _value_pad(ukv[..., QK_NOPE:]).astype(BF16),
        "v_one": jnp.zeros((MLA_HEADS, HEAD_PAD), F32).at[
            jnp.arange(MLA_HEADS), jnp.array([_one_lane(hd) for hd in range(MLA_HEADS)])
        ].set(1.0).reshape(1, MLA_HEADS * HEAD_PAD),
        "gmlp_ln_g": row(gmlp_ln_g),
        "gmlp_ln_b": row(gmlp_ln_b),
        "conv_dw": conv_dw[layer],
        "conv_db": row(conv_db),
        "conv_ln_g": row(conv_ln_g),
        "conv_ln_b": row(conv_ln_b),
        "w_conv_out": w_conv_out[layer].astype(BF16),
        "w_attn_out": w_attn_out[layer].astype(BF16),
        "w_spatial": w_spatial[layer].astype(BF16),
        "b_spatial": jnp.repeat(b_spatial[layer].T, gmlp_dim // groups, axis=1),
        "w_gmlp_out": w_gmlp_out[layer].astype(BF16),
        "w_out": w_out[layer].astype(BF16),
    }


def _rope_tables(n_lat, n_ctx):
    pairs = QK_ROPE // 4
    pos = jnp.arange(n_lat, dtype=jnp.int32)
    row = (pos // GRID_W).astype(F32)
    col = (pos % GRID_W).astype(F32)
    inv_freq = ROPE_THETA ** (-jnp.arange(pairs, dtype=F32) / pairs)
    ar, ac = row[:, None] * inv_freq, col[:, None] * inv_freq

    def table(fn, fill):
        rope = jnp.concatenate([fn(ar), fn(ar), fn(ac), fn(ac)], axis=-1)
        lat = jnp.concatenate([jnp.full((n_lat, QK_NOPE), fill, F32), rope,
                               jnp.full((n_lat, HEAD_PAD - QK_NOPE - QK_ROPE), fill, F32)], axis=-1)
        return jnp.concatenate([lat, jnp.full((n_ctx, HEAD_PAD), fill, F32)], axis=0)

    return table(jnp.cos, 1.0), table(jnp.sin, 0.0)


def kernel(x, c, ctx, c_ctx, w_mod, b_mod, norm_g, w_in, b_gate, conv_dw, conv_db, conv_ln_g, conv_ln_b,
           w_conv_out, q_norm_g, w_uq, kv_norm_g, w_ukv, w_attn_out, gmlp_ln_g, gmlp_ln_b, w_spatial,
           b_spatial, w_gmlp_out, w_out, ffn_w13, ffn_w2, router_w, moe_w13, moe_w2):
    bsz, n_lat, d = x.shape
    n_ctx = ctx.shape[1]
    depth = w_mod.shape[0]
    tm = TOKEN_TILE
    assert n_lat % Q_TILE == 0 and n_lat % n_ctx == 0 and n_ctx % tm == 0 and n_lat % GRID_W == 0
    assert bsz + 1 <= MOD_ROWS and w_spatial.shape[2] == CHUNK and V_HEAD * HEADS_PER_STEP == LANES
    n_lat_tiles = n_lat // tm

    c_rows = jnp.zeros((MOD_ROWS, d), F32).at[:bsz].set(c).at[bsz].set(c_ctx)
    mod_all = _modulation(c_rows, w_mod, b_mod)
    cos_t, sin_t = _rope_tables(n_lat, n_ctx)
    h = jnp.concatenate([x, ctx], axis=1)

    for layer in range(depth):
        need_ctx = layer < depth - 1
        rows_out = n_lat + n_ctx if need_ctx else n_lat
        mod = mod_all[layer].reshape(MOD_ROWS, N_MOD, d)
        modtab = jnp.stack([mod[:bsz], jnp.broadcast_to(mod[bsz], (bsz, N_MOD, d))], axis=1)
        g = norm_g[layer].reshape(4, 1, d)
        wts = _layer_weights(layer, w_in, b_gate, conv_dw, conv_db, conv_ln_g, conv_ln_b, w_conv_out,
                             q_norm_g, w_uq, kv_norm_g, w_ukv, w_attn_out, gmlp_ln_g, gmlp_ln_b, w_spatial,
                             b_spatial, w_gmlp_out, w_out)

        u, q, k, v, gmu, gmv, gates = _inproj(h, modtab, g[0], cos_t, sin_t, wts, n_lat_tiles)
        o_lat = _attention(q, k, v, 0, n_lat, 0, n_lat + n_ctx)
        o_ctx = _attention(q, k, v, n_lat, n_ctx, n_lat, n_ctx) if need_ctx else None

        idx = layer // 2
        router = None
        if layer % 2 == 1:
            n_exp = router_w.shape[2]
            wr = jnp.zeros((d, LANES), F32).at[:, :n_exp].set(router_w[idx])
            wr_hi = wr.astype(BF16)
            router = (wr_hi, (wr - wr_hi.astype(F32)).astype(BF16), n_exp)
        outs = _mixer(h, modtab, u, o_lat, o_ctx, gmu, gmv, gates, wts, g[1], g[2], n_lat_tiles, rows_out,
                      router)
        if router is None:
            h_mid, f_in = outs
            h = _dense_ffn(f_in, h_mid, modtab, g[3], ffn_w13[idx].astype(BF16)[None],
                           ffn_w2[idx].astype(BF16)[None], n_lat_tiles)
        else:
            h_mid, f_in, route = outs
            h = _moe_ffn(f_in, route, h_mid, modtab, g[3], moe_w13[idx].astype(BF16),
                         moe_w2[idx].astype(BF16), n_lat_tiles)
    return h[:, :n_lat]
```

```python
import functools
import math

import jax
import jax.numpy as jnp
from jax import lax
from jax.experimental import pallas as pl
from jax.experimental.pallas import tpu as pltpu

F32 = jnp.float32
BF16 = jnp.bfloat16

GRID_W = 64
MLA_HEADS = 8
QK_NOPE = 64
QK_ROPE = 32
V_HEAD = 64
ROPE_THETA = 10000.0
CHUNK = 128
TOP_K = 2
N_MOD = 6
RMS_EPS = 1e-6
LN_EPS = 1e-5

LANES = 128
F32_SUBLANES = 8
BF16_SUBLANES = 16
VMEM_LIMIT_BYTES = 56 * 1024 * 1024

HEAD_PAD = LANES
HEADS_PER_STEP = LANES // V_HEAD
TOKEN_TILE = 256
HALO = BF16_SUBLANES
CONV_ROWS = 128
Q_TILE = 512
KV_TILE = 256
MOE_ROWS = 512
MOD_ROWS = 16


def _params(*sem):
    return pltpu.CompilerParams(dimension_semantics=sem, vmem_limit_bytes=VMEM_LIMIT_BYTES)


def _resident(shape):
    zeros = (0,) * len(shape)
    return pl.BlockSpec(shape, lambda *_: zeros, pipeline_mode=pl.Buffered(1))


def _dot(a, b):
    return jnp.dot(a, b, preferred_element_type=F32)


def _split_bf16(x):
    hi = x.astype(BF16)
    lo = (x - hi.astype(F32)).astype(BF16)
    return hi, lo


def _dot_f32(x, w_hi, w_lo):
    x_hi, x_lo = _split_bf16(x)
    return _dot(x_hi, w_hi) + (_dot(x_lo, w_hi) + _dot(x_hi, w_lo))


def _rms(x, g):
    return x * lax.rsqrt(jnp.mean(x * x, axis=-1, keepdims=True) + RMS_EPS) * g


def _layer_norm(x, g, b):
    xc = x - jnp.mean(x, axis=-1, keepdims=True)
    var = jnp.mean(xc * xc, axis=-1, keepdims=True)
    return xc * lax.rsqrt(var + LN_EPS) * g + b


def _silu(x):
    return x * jax.nn.sigmoid(x)


def _load_row_tiles(ref, rows):
    return jnp.concatenate([ref[pl.ds(s, rows, stride=F32_SUBLANES), :] for s in range(F32_SUBLANES)], axis=1)


def _store_row_tiles(ref, x):
    for s in range(F32_SUBLANES):
        ref[pl.ds(s, x.shape[0], stride=F32_SUBLANES), :] = x[:, s * LANES:(s + 1) * LANES]


def _mod_kernel(c_ref, w_ref, b_ref, o_ref):
    w_hi, w_lo = _split_bf16(w_ref[0])
    o_ref[0] = _dot_f32(_silu(c_ref[...]), w_hi, w_lo) + b_ref[0]


def _modulation(c_rows, w_mod, b_mod):
    n_layers, d, n_out = w_mod.shape
    return pl.pallas_call(
        _mod_kernel,
        grid=(n_layers, n_out // d),
        in_specs=[
            pl.BlockSpec((MOD_ROWS, d), lambda l, j: (0, 0)),
            pl.BlockSpec((1, d, d), lambda l, j: (l, 0, j)),
            pl.BlockSpec((1, 1, d), lambda l, j: (l, 0, j)),
        ],
        out_specs=pl.BlockSpec((1, MOD_ROWS, d), lambda l, j: (l, 0, j)),
        out_shape=jax.ShapeDtypeStruct((n_layers, MOD_ROWS, n_out), F32),
        compiler_params=_params("arbitrary", "arbitrary"),
        name="modulation",
    )(c_rows, w_mod, b_mod.reshape(n_layers, 1, n_out))


def _depthwise_conv(win_ref, dw_ref, rows, chunk):
    n_taps = dw_ref.shape[0]
    first = HALO - n_taps // 2
    sub = F32_SUBLANES
    span = CONV_ROWS + (first + n_taps - 1) // sub * sub + sub
    cs = slice(chunk * LANES, (chunk + 1) * LANES)
    blocks = []
    for r0 in range(0, rows, CONV_ROWS):
        acc = jnp.zeros((CONV_ROWS, LANES), F32)
        window = win_ref[r0:r0 + span, cs]
        for phase in range(sub):
            taps = [kk for kk in range(n_taps) if (first + kk) % sub == phase]
            if taps:
                ws = window if phase == 0 else pltpu.roll(window, span - phase, axis=0)
                for kk in taps:
                    a = (first + kk) // sub * sub
                    acc = acc + ws[a:a + CONV_ROWS] * dw_ref[kk:kk + 1, cs]
        blocks.append(acc)
    return jnp.concatenate(blocks, axis=0)


def _inproj_kernel(h_ref, hprev_ref, hnext_ref, mod_ref, gpre_ref, cos_ref, sin_ref,
                   w_conv_ref, w_cq_ref, w_ckv_ref, w_kr_ref, w_gm_ref, w_gate_ref, b_gate_ref,
                   qg_ref, wuq_ref, wuq_rot_ref, kvg_ref, wk_ref, wv_ref, vone_ref, gln_g_ref, gln_b_ref,
                   dw_ref, db_ref, cln_g_ref, cln_b_ref,
                   cact_ref, q_ref, k_ref, v_ref, gmu_ref, gmv_ref, gate_ref, win_ref, *,
                   scale, n_lat_tiles, n_tiles):
    t = pl.program_id(1)
    tm = h_ref.shape[1]
    mod = mod_ref[0, 0]

    def modulated(hx):
        return (_rms(hx, gpre_ref[...]) * (1.0 + mod[1:2]) + mod[0:1]).astype(BF16)

    a = modulated(h_ref[0])
    cos = cos_ref[...]
    sin = sin_ref[...]

    a_halo = jnp.concatenate([modulated(hprev_ref[0]), a, modulated(hnext_ref[0])], axis=0)
    zc = _dot(a_halo, w_conv_ref[...])
    half = zc.shape[1] // 2
    u = zc[:, :half] * jax.nn.sigmoid(zc[:, half:])
    prev_ok = jnp.logical_and(t != 0, t != n_lat_tiles)
    next_ok = jnp.logical_and(t != n_lat_tiles - 1, t != n_tiles - 1)
    win_ref[0:HALO, :] = jnp.where(prev_ok, u[0:HALO], 0.0)
    win_ref[HALO:HALO + tm, :] = u[HALO:HALO + tm]
    win_ref[HALO + tm:, :] = jnp.where(next_ok, u[HALO + tm:], 0.0)
    conv = [_depthwise_conv(win_ref, dw_ref, tm, c) for c in range(dw_ref.shape[1] // LANES)]
    conv = jnp.concatenate(conv, axis=-1) + db_ref[...]
    cact_ref[0] = _silu(_layer_norm(conv, cln_g_ref[...], cln_b_ref[...])).astype(BF16)

    cq = _rms(_dot(a, w_cq_ref[...]), qg_ref[...]).astype(BF16)
    qf = _dot(cq, wuq_ref[...])
    qr = _dot(cq, wuq_rot_ref[...])
    for hd in range(qf.shape[1] // HEAD_PAD):
        sl = slice(hd * HEAD_PAD, (hd + 1) * HEAD_PAD)
        q_ref[0, :, sl] = ((qf[:, sl] * cos + qr[:, sl] * sin) * scale).astype(BF16)

    ckv = _rms(_dot(a, w_ckv_ref[...]), kvg_ref[...]).astype(BF16)
    kr = _dot(a, w_kr_ref[...])
    kr = kr[:, :HEAD_PAD] * cos + kr[:, HEAD_PAD:] * sin
    kn = _dot(ckv, wk_ref[...])
    for hd in range(kn.shape[1] // HEAD_PAD):
        sl = slice(hd * HEAD_PAD, (hd + 1) * HEAD_PAD)
        k_ref[0, :, sl] = (kn[:, sl] + kr).astype(BF16)
    v_ref[0] = (_dot(ckv, wv_ref[...]) + vone_ref[...]).astype(BF16)

    zg = jax.nn.gelu(_dot(a, w_gm_ref[...]), approximate=True)
    gd = zg.shape[1] // 2
    gmu_ref[0] = zg[:, :gd].astype(BF16)
    gmv_ref[0] = _layer_norm(zg[:, gd:], gln_g_ref[...], gln_b_ref[...]).astype(BF16)

    gate_ref[0] = jax.nn.sigmoid(_dot(a, w_gate_ref[...]) + b_gate_ref[...]).astype(BF16)


def _inproj(h, modtab, g_pre, cos_t, sin_t, wts, n_lat_tiles):
    bsz, nt, d = h.shape
    tm = TOKEN_TILE
    tok = lambda w: pl.BlockSpec((1, tm, w), lambda b, t: (b, t, 0))
    weights = [wts[k] for k in ("w_conv", "w_cq", "w_ckv", "w_kr", "w_gm", "w_gate", "b_gate", "q_norm_g",
                                "w_uq", "w_uq_rot", "kv_norm_g", "w_k", "w_v", "v_one", "gmlp_ln_g",
                                "gmlp_ln_b", "conv_dw", "conv_db", "conv_ln_g", "conv_ln_b")]
    conv_dim = wts["w_conv"].shape[1] // 2
    widths = (conv_dim, wts["w_uq"].shape[1], wts["w_k"].shape[1], wts["w_v"].shape[1],
              wts["w_gm"].shape[1] // 2, wts["w_gm"].shape[1] // 2, wts["w_gate"].shape[1])
    scale = math.log2(math.e) / math.sqrt(QK_NOPE + QK_ROPE)
    hb = tm // HALO
    last_halo = nt // HALO - 1
    return pl.pallas_call(
        functools.partial(_inproj_kernel, scale=scale, n_lat_tiles=n_lat_tiles, n_tiles=nt // tm),
        grid=(bsz, nt // tm),
        in_specs=[
            tok(d),
            pl.BlockSpec((1, HALO, d), lambda b, t: (b, jnp.maximum(t * hb - 1, 0), 0)),
            pl.BlockSpec((1, HALO, d), lambda b, t: (b, jnp.minimum((t + 1) * hb, last_halo), 0)),
            pl.BlockSpec((1, 1, N_MOD, d), lambda b, t: (b, jnp.where(t >= n_lat_tiles, 1, 0), 0, 0)),
            _resident(g_pre.shape),
            pl.BlockSpec((tm, LANES), lambda b, t: (t, 0)),
            pl.BlockSpec((tm, LANES), lambda b, t: (t, 0)),
        ] + [_resident(w.shape) for w in weights],
        out_specs=[tok(w) for w in widths],
        out_shape=[jax.ShapeDtypeStruct((bsz, nt, w), BF16) for w in widths],
        scratch_shapes=[pltpu.VMEM((tm + 2 * HALO, conv_dim), F32)],
        compiler_params=_params("parallel", "parallel"),
        name="inproj",
    )(h, h, h, modtab, g_pre, cos_t, sin_t, *weights)


def _one_lane(head):
    return ((head % HEADS_PER_STEP + 1) % HEADS_PER_STEP) * V_HEAD


def _attn_kernel(q_ref, k_ref, v_ref, o_ref, *, kv_tile):
    tq = q_ref.shape[1]
    n_kv = k_ref.shape[1] // kv_tile
    heads = [slice(hd * HEAD_PAD, (hd + 1) * HEAD_PAD) for hd in range(HEADS_PER_STEP)]
    qs = [q_ref[0, :, hs] for hs in heads]

    def body(j, carry):
        rows = pl.ds(pl.multiple_of(j * kv_tile, kv_tile), kv_tile)
        out = []
        for hs, q, (m, acc) in zip(heads, qs, carry):
            s = lax.dot_general(q, k_ref[0, rows, hs], (((1,), (1,)), ((), ())),
                                preferred_element_type=F32)
            m_new = jnp.maximum(m, jnp.max(s, axis=-1, keepdims=True))
            p = jnp.exp2(s - m_new).astype(BF16)
            acc = jnp.exp2(m - m_new) * acc + _dot(p, v_ref[0, rows, hs])
            out.append((m_new, acc))
        return tuple(out)

    init = tuple((jnp.full((tq, 1), -1e30, F32), jnp.zeros((tq, LANES), F32)) for _ in heads)
    carry = lax.fori_loop(0, n_kv, body, init, unroll=True)
    lane = lax.broadcasted_iota(jnp.int32, (tq, LANES), 1)
    o = None
    for hd in range(HEADS_PER_STEP - 1, -1, -1):
        acc = carry[hd][1]
        one = _one_lane(hd)
        o_hd = acc / acc[:, one:one + 1]
        o = o_hd if o is None else jnp.where(lane < (hd + 1) * V_HEAD, o_hd, o)
    o_ref[0] = o.astype(BF16)


def _attention(q, k, v, q_start, q_len, kv_start, kv_len):
    bsz = q.shape[0]
    qw = HEADS_PER_STEP * HEAD_PAD
    n_pairs = v.shape[2] // qw
    tq = min(Q_TILE, q_len)
    kv_tile = KV_TILE if kv_len % KV_TILE == 0 else math.gcd(kv_len, KV_TILE)
    q_off, kv_blk = q_start // tq, kv_start // kv_len
    return pl.pallas_call(
        functools.partial(_attn_kernel, kv_tile=kv_tile),
        grid=(bsz, n_pairs, q_len // tq),
        in_specs=[
            pl.BlockSpec((1, tq, qw), lambda b, p, i: (b, q_off + i, p)),
            pl.BlockSpec((1, kv_len, qw), lambda b, p, i: (b, kv_blk, p)),
            pl.BlockSpec((1, kv_len, qw), lambda b, p, i: (b, kv_blk, p)),
        ],
        out_specs=pl.BlockSpec((1, tq, LANES), lambda b, p, i: (b, i, p)),
        out_shape=jax.ShapeDtypeStruct((bsz, q_len, n_pairs * LANES), BF16),
        compiler_params=_params("parallel", "parallel", "arbitrary"),
        name="attention",
    )(q, k, v)


def _mixer_kernel(*refs, n_lat_tiles, has_ctx, moe_next, n_experts):
    (h_ref, mod_ref, cact_ref, ol_ref, oc_ref, gmu_ref, gmv_ref, gate_ref,
     wco_ref, wao_ref, ws_ref, bs_ref, wgo_ref, wout_ref, gpost_ref, gpre_ref) = refs[:16]
    rest = refs[16:]
    if moe_next:
        wr_hi_ref, wr_lo_ref, hout_ref, fin_ref, route_ref = rest
    else:
        hout_ref, fin_ref = rest
    t = pl.program_id(1)
    tm = h_ref.shape[1]
    d = h_ref.shape[2]
    mod = mod_ref[0, 0]

    br_conv = _dot(cact_ref[0], wco_ref[...])

    o = ol_ref[0]
    if has_ctx:
        o = jnp.where(t >= n_lat_tiles, oc_ref[0], o)
    br_attn = _dot(o, wao_ref[...])

    gmv = gmv_ref[0]
    lane = lax.broadcasted_iota(jnp.int32, (CHUNK, LANES), 1)
    group_dim = gmv.shape[1] // ws_ref.shape[0]
    rows = []
    for j in range(tm // CHUNK):
        cols = []
        for gp in range(gmv.shape[1] // LANES):
            vp = gmv[j * CHUNK:(j + 1) * CHUNK, gp * LANES:(gp + 1) * LANES]
            s0 = _dot(ws_ref[2 * gp], vp)
            s1 = _dot(ws_ref[2 * gp + 1], vp)
            cols.append(jnp.where(lane < group_dim, s0, s1))
        rows.append(jnp.concatenate(cols, axis=-1) + bs_ref[...])
    gm = (gmu_ref[0].astype(F32) * jnp.concatenate(rows, axis=0)).astype(BF16)
    br_gmlp = _dot(gm, wgo_ref[...])

    g = gate_ref[0]
    mix = (g[:, :d] * br_conv.astype(BF16) + g[:, d:2 * d] * br_attn.astype(BF16)
           + g[:, 2 * d:] * br_gmlp.astype(BF16))
    y = _dot(mix, wout_ref[...])
    h_new = h_ref[0] + mod[2:3] * _rms(y, gpost_ref[...])
    hout_ref[0] = h_new
    f_in = _rms(h_new, gpre_ref[...]) * (1.0 + mod[4:5]) + mod[3:4]
    if not moe_next:
        fin_ref[0] = f_in.astype(fin_ref.dtype)
    else:
        _store_row_tiles(fin_ref.at[0], f_in)
        logits = _dot_f32(f_in, wr_hi_ref[...], wr_lo_ref[...])
        ln = lax.broadcasted_iota(jnp.int32, logits.shape, 1)
        neg = -jnp.inf
        lg = jnp.where(ln < n_experts, logits, neg)
        m1 = jnp.max(lg, axis=-1, keepdims=True)
        i1 = jnp.min(jnp.where(lg == m1, ln, LANES), axis=-1, keepdims=True)
        lg2 = jnp.where(ln == i1, neg, lg)
        m2 = jnp.max(lg2, axis=-1, keepdims=True)
        i2 = jnp.min(jnp.where(lg2 == m2, ln, LANES), axis=-1, keepdims=True)
        e2 = jnp.exp(m2 - m1)
        w1 = 1.0 / (1.0 + e2)
        w2 = e2 / (1.0 + e2)
        route = jnp.where(ln == 0, i1.astype(F32),
                          jnp.where(ln == 1, i2.astype(F32),
                                    jnp.where(ln == 2, w1, jnp.where(ln == 3, w2, 0.0))))
        route_ref[0] = route


def _mixer(h, modtab, c_act, o_lat, o_ctx, gmu, gmv, gates, wts, g_post, g_pre_f, n_lat_tiles, rows_out,
           router):
    bsz, nt, d = h.shape
    tm = TOKEN_TILE
    has_ctx = o_ctx is not None
    moe_next = router is not None
    if not has_ctx:
        o_ctx = o_lat
    n_ctx_blocks = o_ctx.shape[1] // tm
    tok = lambda w: pl.BlockSpec((1, tm, w), lambda b, t: (b, t, 0))
    weights = [wts[k] for k in ("w_conv_out", "w_attn_out", "w_spatial", "b_spatial", "w_gmlp_out",
                                "w_out")] + [g_post, g_pre_f]
    in_specs = [
        tok(d),
        pl.BlockSpec((1, 1, N_MOD, d), lambda b, t: (b, jnp.where(t >= n_lat_tiles, 1, 0), 0, 0)),
        tok(c_act.shape[2]),
        pl.BlockSpec((1, tm, o_lat.shape[2]), lambda b, t: (b, jnp.minimum(t, n_lat_tiles - 1), 0)),
        pl.BlockSpec((1, tm, o_ctx.shape[2]),
                     lambda b, t: (b, jnp.clip(t - n_lat_tiles, 0, n_ctx_blocks - 1), 0)),
        tok(gmu.shape[2]), tok(gmv.shape[2]), tok(gates.shape[2]),
    ] + [_resident(w.shape) for w in weights]
    operands = [h, modtab, c_act, o_lat, o_ctx, gmu, gmv, gates] + weights
    out_specs = [tok(d), tok(d)]
    out_shape = [jax.ShapeDtypeStruct((bsz, rows_out, d), F32), jax.ShapeDtypeStruct((bsz, rows_out, d), BF16)]
    n_experts = 0
    if moe_next:
        assert d == F32_SUBLANES * LANES
        out_specs[1] = pl.BlockSpec((1, tm * F32_SUBLANES, LANES), lambda b, t: (b, t, 0))
        out_shape[1] = jax.ShapeDtypeStruct((bsz, rows_out * F32_SUBLANES, LANES), F32)
        n_experts = router[2]
        in_specs += [_resident(router[0].shape), _resident(router[1].shape)]
        operands += [router[0], router[1]]
        out_specs.append(tok(LANES))
        out_shape.append(jax.ShapeDtypeStruct((bsz, rows_out, LANES), F32))
    return pl.pallas_call(
        functools.partial(_mixer_kernel, n_lat_tiles=n_lat_tiles, has_ctx=has_ctx, moe_next=moe_next,
                          n_experts=n_experts),
        grid=(bsz, rows_out // tm),
        in_specs=in_specs,
        out_specs=out_specs,
        out_shape=out_shape,
        compiler_params=_params("parallel", "parallel"),
        name="mixer",
    )(*operands)


def _swiglu_block(x, w13_ref, w2_ref, ff_tile, after_matmul=None):
    d_ff = w2_ref.shape[1]
    n_chunks = d_ff // ff_tile
    done = [0]

    def dot_then_side_work(a, b):
        out = _dot(a, b)
        if after_matmul is not None:
            after_matmul(done[0], 3 * n_chunks)
        done[0] += 1
        return out

    acc = jnp.zeros((x.shape[0], w2_ref.shape[2]), F32)
    for j in range(n_chunks):
        a = dot_then_side_work(x, w13_ref[0, :, j * ff_tile:(j + 1) * ff_tile])
        g = dot_then_side_work(x, w13_ref[0, :, d_ff + j * ff_tile:d_ff + (j + 1) * ff_tile])
        acc = acc + dot_then_side_work((_silu(a) * g).astype(BF16), w2_ref[0, j * ff_tile:(j + 1) * ff_tile, :])
    return acc


def _dense_ffn_kernel(x_ref, h_ref, mod_ref, gpost_ref, w13_ref, w2_ref, o_ref, *, ff_tile):
    f = _swiglu_block(x_ref[0], w13_ref, w2_ref, ff_tile)
    o_ref[0] = h_ref[0] + mod_ref[0, 0][5:6] * _rms(f, gpost_ref[...])


def _ff_tile(d_ff, target):
    n = d_ff // LANES
    best = 1
    for c in range(1, n + 1):
        if n % c == 0 and c * LANES <= target:
            best = c
    return best * LANES


def _dense_ffn(f_in, h, modtab, g_post, w13, w2, n_lat_tiles):
    bsz, rows, d = h.shape
    tm = TOKEN_TILE
    tok = lambda w: pl.BlockSpec((1, tm, w), lambda b, t: (b, t, 0))
    return pl.pallas_call(
        functools.partial(_dense_ffn_kernel, ff_tile=w2.shape[1]),
        grid=(bsz, rows // tm),
        in_specs=[
            tok(d), tok(d),
            pl.BlockSpec((1, 1, N_MOD, d), lambda b, t: (b, jnp.where(t >= n_lat_tiles, 1, 0), 0, 0)),
            _resident(g_post.shape), _resident(w13.shape), _resident(w2.shape),
        ],
        out_specs=tok(d),
        out_shape=jax.ShapeDtypeStruct((bsz, rows, d), F32),
        compiler_params=_params("parallel", "parallel"),
        name="dense_ffn",
    )(f_in, h, modtab, g_post, w13, w2)


def _expert_ffn_kernel(be_ref, src_ref, src_next_ref, dst_prev_ref, dst_ref, x_hbm, w13_ref, w2_ref, y_hbm,
                       xbuf, ybuf, gsem, ssem, *, ff_tile, n_blocks):
    del be_ref
    i = pl.program_id(0)
    rows = src_ref.shape[2]
    slot = i % 2
    other = 1 - slot
    sub = F32_SUBLANES

    def gather_row(idx_ref, buf_slot, r):
        pltpu.make_async_copy(x_hbm.at[pl.ds(idx_ref[0, 0, r] * sub, sub)],
                              xbuf.at[buf_slot, pl.ds(r * sub, sub)], gsem.at[buf_slot]).start()

    def scatter_row(idx_ref, buf_slot, r):
        pltpu.make_async_copy(ybuf.at[buf_slot, pl.ds(r * sub, sub)],
                              y_hbm.at[pl.ds(idx_ref[0, 0, r] * sub, sub)], ssem.at[buf_slot]).start()

    def all_rows(row_fn):
        def body(r, carry):
            row_fn(r)
            return carry
        lax.fori_loop(0, rows, body, 0)

    def wait_rows(hbm, buf, sem, buf_slot):
        pltpu.make_async_copy(hbm.at[pl.ds(0, rows * sub)], buf.at[buf_slot], sem.at[buf_slot]).wait()

    @pl.when(i == 0)
    def _():
        all_rows(functools.partial(gather_row, src_ref, 0))
        ybuf[1] = jnp.zeros(ybuf.shape[1:], F32)

    wait_rows(x_hbm, xbuf, gsem, slot)

    def side_copies(k, n):
        per = -(-rows // n)
        for r in range(k * per, min((k + 1) * per, rows)):
            gather_row(src_next_ref, other, r)
            scatter_row(dst_prev_ref, other, r)

    x = _load_row_tiles(xbuf.at[slot], rows).astype(BF16)
    y = _swiglu_block(x, w13_ref, w2_ref, ff_tile, side_copies)

    @pl.when(i >= 1)
    def _():
        wait_rows(y_hbm, ybuf, ssem, slot)

    _store_row_tiles(ybuf.at[slot], y)

    @pl.when(i == n_blocks - 1)
    def _():
        all_rows(functools.partial(scatter_row, dst_ref, slot))
        wait_rows(x_hbm, xbuf, gsem, other)
        wait_rows(y_hbm, ybuf, ssem, other)
        wait_rows(y_hbm, ybuf, ssem, slot)


def _expert_ffn(x_tiles, block_expert, src, dst, w13, w2):
    n_rows = src.shape[0]
    rb = MOE_ROWS
    n_blocks = n_rows // rb
    _, d, two_f = w13.shape
    d_ff = w2.shape[1]
    sub = F32_SUBLANES
    idx_spec = lambda f: pl.BlockSpec((1, 1, rb), f, memory_space=pltpu.SMEM)
    dst_blocks = jnp.concatenate([n_rows + jnp.arange(rb, dtype=jnp.int32), dst]).reshape(n_blocks + 1, 1, rb)
    grid_spec = pltpu.PrefetchScalarGridSpec(
        num_scalar_prefetch=1,
        grid=(n_blocks,),
        in_specs=[
            idx_spec(lambda i, be: (i, 0, 0)),
            idx_spec(lambda i, be: (jnp.minimum(i + 1, n_blocks - 1), 0, 0)),
            idx_spec(lambda i, be: (i, 0, 0)),
            idx_spec(lambda i, be: (i + 1, 0, 0)),
            pl.BlockSpec(memory_space=pl.ANY),
            pl.BlockSpec((1, d, two_f), lambda i, be: (be[i], 0, 0), pipeline_mode=pl.Buffered(1)),
            pl.BlockSpec((1, d_ff, d), lambda i, be: (be[i], 0, 0), pipeline_mode=pl.Buffered(1)),
        ],
        out_specs=pl.BlockSpec(memory_space=pl.ANY),
        scratch_shapes=[
            pltpu.VMEM((2, rb * sub, LANES), F32),
            pltpu.VMEM((2, rb * sub, LANES), F32),
            pltpu.SemaphoreType.DMA((2,)),
            pltpu.SemaphoreType.DMA((2,)),
        ],
    )
    src3 = src.reshape(n_blocks, 1, rb)
    return pl.pallas_call(
        functools.partial(_expert_ffn_kernel, ff_tile=_ff_tile(d_ff, 512), n_blocks=n_blocks),
        grid_spec=grid_spec,
        out_shape=jax.ShapeDtypeStruct(((n_rows + rb) * sub, LANES), F32),
        compiler_params=_params("arbitrary"),
        name="expert_ffn",
    )(block_expert, src3, src3, dst_blocks, dst_blocks, x_tiles, w13, w2)


def _combine_kernel(y0_ref, y1_ref, route_ref, h_ref, mod_ref, gpost_ref, o_ref):
    route = route_ref[0]
    tm = route.shape[0]
    f = route[:, 2:3] * _load_row_tiles(y0_ref, tm) + route[:, 3:4] * _load_row_tiles(y1_ref, tm)
    o_ref[0] = h_ref[0] + mod_ref[0, 0][5:6] * _rms(f, gpost_ref[...])


def _combine(y_tiles, route, h, modtab, g_post, n_lat_tiles):
    bsz, rows, d = h.shape
    tm = TOKEN_TILE
    tpb = rows // tm
    second = bsz * tpb
    tok = lambda w: pl.BlockSpec((1, tm, w), lambda b, t: (b, t, 0))
    return pl.pallas_call(
        _combine_kernel,
        grid=(bsz, tpb),
        in_specs=[
            pl.BlockSpec((tm * F32_SUBLANES, LANES), lambda b, t: (b * tpb + t, 0)),
            pl.BlockSpec((tm * F32_SUBLANES, LANES), lambda b, t: (second + b * tpb + t, 0)),
            tok(LANES), tok(d),
            pl.BlockSpec((1, 1, N_MOD, d), lambda b, t: (b, jnp.where(t >= n_lat_tiles, 1, 0), 0, 0)),
            _resident(g_post.shape),
        ],
        out_specs=tok(d),
        out_shape=jax.ShapeDtypeStruct((bsz, rows, d), F32),
        compiler_params=_params("parallel", "parallel"),
        name="moe_combine",
    )(y_tiles, y_tiles, route, h, modtab, g_post)


def _moe_ffn(f_tiles, route, h, modtab, g_post, w13, w2, n_lat_tiles):
    bsz, rows, d = h.shape
    n_tok = bsz * rows
    n_exp = w13.shape[0]
    n_assign = n_tok * TOP_K
    rb = MOE_ROWS
    expert = route[..., :TOP_K].astype(jnp.int32).reshape(n_assign)
    counts = jnp.sum((expert[:, None] == jnp.arange(n_exp, dtype=jnp.int32)[None, :]).astype(jnp.int32), axis=0)
    ends = jnp.cumsum(counts)
    padded = (counts + rb - 1) // rb * rb
    padded_end = jnp.cumsum(padded)
    n_rows = n_assign + n_exp * rb
    n_blocks = n_rows // rb
    _, order = lax.sort((expert, jnp.arange(n_assign, dtype=jnp.int32)), num_keys=1, is_stable=True)
    block_start = jnp.arange(n_blocks, dtype=jnp.int32) * rb
    block_expert = jnp.minimum(
        jnp.sum((padded_end[None, :] <= block_start[:, None]).astype(jnp.int32), axis=1), n_exp - 1)
    row = block_start[:, None] + jnp.arange(rb, dtype=jnp.int32)[None, :]
    rank = row - (padded_end - padded)[block_expert][:, None]
    is_pad = rank >= counts[block_expert][:, None]
    held = order[jnp.clip((ends - counts)[block_expert][:, None] + rank, 0, n_assign - 1)]
    src = jnp.where(is_pad, 0, held // TOP_K).reshape(n_rows)
    spare = n_assign + row - ends[block_expert][:, None]
    dst = jnp.where(is_pad, spare, (held % TOP_K) * n_tok + held // TOP_K).reshape(n_rows)

    x_tiles = f_tiles.reshape(n_tok * F32_SUBLANES, LANES)
    y_tiles = _expert_ffn(x_tiles, block_expert, src, dst, w13, w2)
    return _combine(y_tiles, route, h, modtab, g_post, n_lat_tiles)


def _rot_cols(r):
    q = QK_ROPE // 4
    return jnp.concatenate([-r[..., q:2 * q], r[..., :q], -r[..., 3 * q:], r[..., 2 * q:3 * q]], axis=-1)


def _head_pad(nope, rope):
    pad = jnp.zeros(nope.shape[:-1] + (HEAD_PAD - QK_NOPE - QK_ROPE,), nope.dtype)
    out = jnp.concatenate([nope, rope, pad], axis=-1)
    return out.reshape(out.shape[:-2] + (out.shape[-2] * HEAD_PAD,))


def _value_pad(wv):
    k, n_heads, _ = wv.shape
    tiles = []
    for hd in range(n_heads):
        pos = hd % HEADS_PER_STEP
        tiles.append(jnp.pad(wv[:, hd], ((0, 0), (pos * V_HEAD, HEAD_PAD - (pos + 1) * V_HEAD))))
    return jnp.concatenate(tiles, axis=1)


def _layer_weights(layer, w_in, b_gate, conv_dw, conv_db, conv_ln_g, conv_ln_b, w_conv_out, q_norm_g, w_uq,
                   kv_norm_g, w_ukv, w_attn_out, gmlp_ln_g, gmlp_ln_b, w_spatial, b_spatial, w_gmlp_out,
                   w_out):
    d = w_in.shape[1]
    conv_dim = conv_dw.shape[2]
    q_lora, kv_lora = w_uq.shape[1], w_ukv.shape[1]
    gmlp_dim = w_gmlp_out.shape[1]
    col_q = 2 * conv_dim
    col_kv = col_q + q_lora
    col_kr = col_kv + kv_lora
    col_gm = col_kr + QK_ROPE
    col_gate = col_gm + 2 * gmlp_dim
    wi = w_in[layer]
    row = lambda v: v[layer].reshape(1, -1)

    uq = w_uq[layer].reshape(q_lora, MLA_HEADS, QK_NOPE + QK_ROPE)
    uq_nope, uq_rope = uq[..., :QK_NOPE], uq[..., QK_NOPE:]
    ukv = w_ukv[layer].reshape(kv_lora, MLA_HEADS, QK_NOPE + V_HEAD)
    wkr = wi[:, col_kr:col_gm]
    zeros_nope = jnp.zeros((d, 1, QK_NOPE), F32)
    groups = w_spatial.shape[1]
    return {
        "w_conv": wi[:, :col_q].astype(BF16),
        "w_cq": wi[:, col_q:col_kv].astype(BF16),
        "w_ckv": wi[:, col_kv:col_kr].astype(BF16),
        "w_kr": jnp.concatenate([_head_pad(zeros_nope, wkr[:, None, :]),
                                 _head_pad(zeros_nope, _rot_cols(wkr)[:, None, :])], axis=1).astype(BF16),
        "w_gm": wi[:, col_gm:col_gate].astype(BF16),
        "w_gate": wi[:, col_gate:].astype(BF16),
        "b_gate": row(b_gate),
        "q_norm_g": row(q_norm_g),
        "w_uq": _head_pad(uq_nope, uq_rope).astype(BF16),
        "w_uq_rot": _head_pad(jnp.zeros_like(uq_nope), _rot_cols(uq_rope)).astype(BF16),
        "kv_norm_g": row(kv_norm_g),
        "w_k": _head_pad(ukv[..., :QK_NOPE], jnp.zeros((kv_lora, MLA_HEADS, QK_ROPE), F32)).astype(BF16),
        "w_v": _value_pad(ukv[..., QK_NOPE:]).astype(BF16),
        "v_one": jnp.zeros((MLA_HEADS, HEAD_PAD), F32).at[
            jnp.arange(MLA_HEADS), jnp.array([_one_lane(hd) for hd in range(MLA_HEADS)])
        ].set(1.0).reshape(1, MLA_HEADS * HEAD_PAD),
        "gmlp_ln_g": row(gmlp_ln_g),
        "gmlp_ln_b": row(gmlp_ln_b),
        "conv_dw": conv_dw[layer],
        "conv_db": row(conv_db),
        "conv_ln_g": row(conv_ln_g),
        "conv_ln_b": row(conv_ln_b),
        "w_conv_out": w_conv_out[layer].astype(BF16),
        "w_attn_out": w_attn_out[layer].astype(BF16),
        "w_spatial": w_spatial[layer].astype(BF16),
        "b_spatial": jnp.repeat(b_spatial[layer].T, gmlp_dim // groups, axis=1),
        "w_gmlp_out": w_gmlp_out[layer].astype(BF16),
        "w_out": w_out[layer].astype(BF16),
    }


def _rope_tables(n_lat, n_ctx):
    pairs = QK_ROPE // 4
    pos = jnp.arange(n_lat, dtype=jnp.int32)
    row = (pos // GRID_W).astype(F32)
    col = (pos % GRID_W).astype(F32)
    inv_freq = ROPE_THETA ** (-jnp.arange(pairs, dtype=F32) / pairs)
    ar, ac = row[:, None] * inv_freq, col[:, None] * inv_freq

    def table(fn, fill):
        rope = jnp.concatenate([fn(ar), fn(ar), fn(ac), fn(ac)], axis=-1)
        lat = jnp.concatenate([jnp.full((n_lat, QK_NOPE), fill, F32), rope,
                               jnp.full((n_lat, HEAD_PAD - QK_NOPE - QK_ROPE), fill, F32)], axis=-1)
        return jnp.concatenate([lat, jnp.full((n_ctx, HEAD_PAD), fill, F32)], axis=0)

    return table(jnp.cos, 1.0), table(jnp.sin, 0.0)


def kernel(x, c, ctx, c_ctx, w_mod, b_mod, norm_g, w_in, b_gate, conv_dw, conv_db, conv_ln_g, conv_ln_b,
           w_conv_out, q_norm_g, w_uq, kv_norm_g, w_ukv, w_attn_out, gmlp_ln_g, gmlp_ln_b, w_spatial,
           b_spatial, w_gmlp_out, w_out, ffn_w13, ffn_w2, router_w, moe_w13, moe_w2):
    bsz, n_lat, d = x.shape
    n_ctx = ctx.shape[1]
    depth = w_mod.shape[0]
    tm = TOKEN_TILE
    assert n_lat % Q_TILE == 0 and n_lat % n_ctx == 0 and n_ctx % tm == 0 and n_lat % GRID_W == 0
    assert bsz + 1 <= MOD_ROWS and w_spatial.shape[2] == CHUNK and V_HEAD * HEADS_PER_STEP == LANES
    n_lat_tiles = n_lat // tm

    c_rows = jnp.zeros((MOD_ROWS, d), F32).at[:bsz].set(c).at[bsz].set(c_ctx)
    mod_all = _modulation(c_rows, w_mod, b_mod)
    cos_t, sin_t = _rope_tables(n_lat, n_ctx)
    h = jnp.concatenate([x, ctx], axis=1)

    for layer in range(depth):
        need_ctx = layer < depth - 1
        rows_out = n_lat + n_ctx if need_ctx else n_lat
        mod = mod_all[layer].reshape(MOD_ROWS, N_MOD, d)
        modtab = jnp.stack([mod[:bsz], jnp.broadcast_to(mod[bsz], (bsz, N_MOD, d))], axis=1)
        g = norm_g[layer].reshape(4, 1, d)
        wts = _layer_weights(layer, w_in, b_gate, conv_dw, conv_db, conv_ln_g, conv_ln_b, w_conv_out,
                             q_norm_g, w_uq, kv_norm_g, w_ukv, w_attn_out, gmlp_ln_g, gmlp_ln_b, w_spatial,
                             b_spatial, w_gmlp_out, w_out)

        u, q, k, v, gmu, gmv, gates = _inproj(h, modtab, g[0], cos_t, sin_t, wts, n_lat_tiles)
        o_lat = _attention(q, k, v, 0, n_lat, 0, n_lat + n_ctx)
        o_ctx = _attention(q, k, v, n_lat, n_ctx, n_lat, n_ctx) if need_ctx else None

        idx = layer // 2
        router = None
        if layer % 2 == 1:
            n_exp = router_w.shape[2]
            wr = jnp.zeros((d, LANES), F32).at[:, :n_exp].set(router_w[idx])
            wr_hi = wr.astype(BF16)
            router = (wr_hi, (wr - wr_hi.astype(F32)).astype(BF16), n_exp)
        outs = _mixer(h, modtab, u, o_lat, o_ctx, gmu, gmv, gates, wts, g[1], g[2], n_lat_tiles, rows_out,
                      router)
        if router is None:
            h_mid, f_in = outs
            h = _dense_ffn(f_in, h_mid, modtab, g[3], ffn_w13[idx].astype(BF16)[None],
                           ffn_w2[idx].astype(BF16)[None], n_lat_tiles)
        else:
            h_mid, f_in, route = outs
            h = _moe_ffn(f_in, route, h_mid, modtab, g[3], moe_w13[idx].astype(BF16),
                         moe_w2[idx].astype(BF16), n_lat_tiles)
    return h[:, :n_lat]
```
